```python
import math
import jax, jax.numpy as jnp
from jax import lax
import numpy as np

D_MODEL = 1024
BATCH = 8
SEQ = 8192
DEPTH = 1

N_META = 16
ATTN_HEADS = 4
ATTN_HEAD_DIM = 64
ATTN_VALUE_DIM = 2 * ATTN_HEAD_DIM
ATTN_WIDTH = ATTN_HEADS * ATTN_VALUE_DIM
CONV_WIDTH = D_MODEL - ATTN_WIDTH
CONV_K = 3
MIX_WIDTH = ATTN_WIDTH + CONV_WIDTH
IN_PROJ_WIDTH = 3 * ATTN_WIDTH + 3 * CONV_WIDTH
N_EXPERTS = 16
EC_CAPACITY_FACTOR = 2
D_FF_EXPERT = 2 * D_MODEL
ROPE_THETA = 10000.0
Q_BLOCK = 128
NORM_EPS = 1e-6

kernel_name = "hymba_diffattn_shortconv_ec_moe"


def rms_norm(x, g):
    xf = x.astype(jnp.float32)
    y = xf * lax.rsqrt(jnp.mean(xf * xf, axis=-1, keepdims=True) + NORM_EPS)
    return (y * g.astype(jnp.float32)).astype(x.dtype)


def rope_tables(n):
    d = ATTN_HEAD_DIM
    inv_freq = ROPE_THETA ** (-jnp.arange(0, d, 2, dtype=jnp.float32) / d)
    ang = jnp.arange(n, dtype=jnp.float32)[:, None] * inv_freq[None, :]
    ang = jnp.concatenate([ang, ang], axis=-1)
    return jnp.cos(ang), jnp.sin(ang)


def apply_rope(x, cos, sin):
    half = x.shape[-1] // 2
    x1, x2 = x[..., :half], x[..., half:]
    rot = jnp.concatenate([-x2, x1], axis=-1)
    return (x * cos + rot * sin).astype(x.dtype)


def diff_attention_group(q, k, v, lq1, lk1, lq2, lk2, subln_g, lam_init, cos, sin):
    B, L, _ = q.shape
    Lp = cos.shape[0]
    H, d = ATTN_HEADS, ATTN_HEAD_DIM
    pad = Lp - L
    q = jnp.pad(q.reshape(B, L, H, 2, d), ((0, 0), (0, pad), (0, 0), (0, 0), (0, 0)))
    k = jnp.pad(k.reshape(B, L, H, 2, d), ((0, 0), (0, pad), (0, 0), (0, 0), (0, 0)))
    v = jnp.pad(v.reshape(B, L, H, 2 * d), ((0, 0), (0, pad), (0, 0), (0, 0)))
    q = apply_rope(q.transpose(0, 2, 3, 1, 4), cos, sin)
    k = apply_rope(k.transpose(0, 2, 3, 1, 4), cos, sin)
    v32 = v.transpose(0, 2, 1, 3).astype(jnp.float32)

    lam = (jnp.exp(jnp.sum(lq1.astype(jnp.float32) * lk1.astype(jnp.float32)))
           - jnp.exp(jnp.sum(lq2.astype(jnp.float32) * lk2.astype(jnp.float32)))
           + lam_init)
    key_valid = jnp.arange(Lp) < L
    scale = d ** -0.5
    n_blk = Lp // Q_BLOCK
    qb = jnp.moveaxis(q.reshape(B, H, 2, n_blk, Q_BLOCK, d), 3, 0)

    def one_block(qi):
        s = jnp.einsum('bhiqd,bhikd->bhiqk', qi, k,
                       preferred_element_type=jnp.float32) * scale
        s = jnp.where(key_valid, s, jnp.finfo(jnp.float32).min)
        p = jax.nn.softmax(s, axis=-1)
        a = p[:, :, 0] - lam * p[:, :, 1]
        return jnp.einsum('bhqk,bhkv->bhqv', a, v32)

    o = lax.map(one_block, qb)
    o = o.transpose(1, 0, 3, 2, 4).reshape(B, Lp, H, 2 * d)[:, :L]
    o = rms_norm(o, subln_g) * (1.0 - lam_init)
    return o.reshape(B, L, ATTN_WIDTH).astype(q.dtype)


def centred_depthwise_conv(u, w):
    L = u.shape[1]
    half = CONV_K // 2
    up = jnp.pad(u, ((0, 0), (half, half), (0, 0)))
    return sum(w[j] * up[:, j:j + L] for j in range(CONV_K))


def expert_choice_ffn(h, w_router, w_gate, w_up, w_down):
    B, L, D = h.shape
    cap = EC_CAPACITY_FACTOR * L // N_EXPERTS

    def one_group(hs):
        logits = jnp.einsum('ld,de->le', hs, w_router, preferred_element_type=jnp.float32)
        aff = jax.nn.softmax(logits, axis=-1)
        g, idx = lax.top_k(aff.T, cap)
        xg = hs[idx]
        a = jnp.einsum('ecd,edf->ecf', xg, w_gate)
        u = jnp.einsum('ecd,edf->ecf', xg, w_up)
        y = jnp.einsum('ecf,efd->ecd', jax.nn.silu(a) * u, w_down)
        y = y * g[..., None].astype(y.dtype)
        out = jnp.zeros((L, D), y.dtype).at[idx.reshape(-1)].add(y.reshape(-1, D))
        return out.astype(hs.dtype)

    return lax.map(one_group, h)


def setup_inputs(seed: int = 0) -> dict:
    key = jax.random.key(seed)
    ks = jax.random.split(key, 20)
    f32 = jnp.float32
    nrm = lambda k, shape, s: jax.random.normal(k, shape, f32) * s
    gain = lambda k, shape: 1.0 + 0.02 * jax.random.normal(k, shape, f32)
    return {
        "x": nrm(ks[0], (BATCH, SEQ, D_MODEL), 1.0),
        "meta_tokens": nrm(ks[1], (N_META, D_MODEL), 1.0),
        "mix_norm_g": gain(ks[2], (DEPTH, D_MODEL)),
        "w_in": nrm(ks[3], (DEPTH, D_MODEL, IN_PROJ_WIDTH), D_MODEL ** -0.5),
        "conv_w": nrm(ks[4], (DEPTH, CONV_K, CONV_WIDTH), CONV_K ** -0.5),
        "lambda_q1": nrm(ks[5], (DEPTH, ATTN_HEAD_DIM), 0.1),
        "lambda_k1": nrm(ks[6], (DEPTH, ATTN_HEAD_DIM), 0.1),
        "lambda_q2": nrm(ks[7], (DEPTH, ATTN_HEAD_DIM), 0.1),
        "lambda_k2": nrm(ks[8], (DEPTH, ATTN_HEAD_DIM), 0.1),
        "attn_subln_g": gain(ks[9], (DEPTH, ATTN_VALUE_DIM)),
        "w_out": nrm(ks[10], (DEPTH, MIX_WIDTH, D_MODEL), MIX_WIDTH ** -0.5),
        "ffn_norm_g": gain(ks[11], (DEPTH, D_MODEL)),
        "w_router": nrm(ks[12], (DEPTH, D_MODEL, N_EXPERTS), D_MODEL ** -0.5),
        "w_gate": nrm(ks[13], (DEPTH, N_EXPERTS, D_MODEL, D_FF_EXPERT), D_MODEL ** -0.5),
        "w_up": nrm(ks[14], (DEPTH, N_EXPERTS, D_MODEL, D_FF_EXPERT), D_MODEL ** -0.5),
        "w_down": nrm(ks[15], (DEPTH, N_EXPERTS, D_FF_EXPERT, D_MODEL), D_FF_EXPERT ** -0.5),
        "final_norm_g": gain(ks[16], (D_MODEL,)),
    }


def reference(x, meta_tokens, mix_norm_g, w_in, conv_w, lambda_q1, lambda_k1, lambda_q2,
              lambda_k2, attn_subln_g, w_out, ffn_norm_g, w_router, w_gate, w_up, w_down,
              final_norm_g):
    B = x.shape[0]
    meta = jnp.broadcast_to(meta_tokens.astype(x.dtype)[None], (B, N_META, D_MODEL))
    h = jnp.concatenate([meta, x], axis=1)
    L = h.shape[1]
    Lp = -(-L // Q_BLOCK) * Q_BLOCK
    cos, sin = rope_tables(Lp)
    splits = [ATTN_WIDTH, 2 * ATTN_WIDTH, 3 * ATTN_WIDTH,
              3 * ATTN_WIDTH + CONV_WIDTH, 3 * ATTN_WIDTH + 2 * CONV_WIDTH]

    for l in range(DEPTH):
        lam_init = 0.8 - 0.6 * math.exp(-0.3 * l)
        hn = rms_norm(h, mix_norm_g[l])
        proj = jnp.einsum('bld,dp->blp', hn, w_in[l])
        q, k, v, cx, cb, cc = jnp.split(proj, splits, axis=-1)
        attn = diff_attention_group(q, k, v, lambda_q1[l], lambda_k1[l], lambda_q2[l],
                                    lambda_k2[l], attn_subln_g[l], lam_init, cos, sin)
        conv = cb * centred_depthwise_conv(cc * cx, conv_w[l])
        mixed = jnp.concatenate([attn, conv.astype(attn.dtype)], axis=-1)
        h = h + jnp.einsum('blm,md->bld', mixed, w_out[l])
        h = h + expert_choice_ffn(rms_norm(h, ffn_norm_g[l]), w_router[l], w_gate[l],
                                  w_up[l], w_down[l])

    y = rms_norm(h, final_norm_g)
    return y[:, N_META:, :]
```

```python
import functools

import jax
import jax.numpy as jnp
from jax import lax
from jax.experimental import pallas as pl
from jax.experimental.pallas import tpu as pltpu

N_META = 16
EC_CAPACITY_FACTOR = 2
ROPE_THETA = 10000.0
NORM_EPS = 1e-6
LAM_INIT = 0.8 - 0.6 * 1.0

LANE = 128
SUBLANE = 8
BF16_ROWS = 16
VMEM_LIMIT = 56 * 1024 * 1024

ROW_TILE = 768
ONES_ROWS = BF16_ROWS
MASK_VALUE = -1e30
GATHER_UNROLL = 8

f32 = jnp.float32
bf16 = jnp.bfloat16
i32 = jnp.int32


def _round_up(a, m):
    return -(-a // m) * m


def _params(*sem):
    return pltpu.CompilerParams(dimension_semantics=sem, vmem_limit_bytes=VMEM_LIMIT)


def _rms(x, g):
    return x * lax.rsqrt(jnp.mean(x * x, axis=-1, keepdims=True) + NORM_EPS) * g


def _inproj_kernel(h_ref, g_ref, w_ref, cos_ref, sin_ref, qT_ref, k_ref, vT_ref, u_ref, cb_ref,
                   *, aw, cw, hd):
    tm = h_ref.shape[1]
    hn = _rms(h_ref[0], g_ref[...]).astype(bf16)
    proj = jnp.dot(hn, w_ref[...], preferred_element_type=f32)
    cos = cos_ref[...]
    sin = sin_ref[...]
    lane = lax.broadcasted_iota(i32, (tm, LANE), 1)
    first_half = (lane % hd) < (hd // 2)

    def rope(x):
        rot = jnp.where(first_half, pltpu.roll(x, LANE - hd // 2, 1), pltpu.roll(x, hd // 2, 1))
        return x * cos + rot * sin

    scale = hd ** -0.5
    vrows = LANE + ONES_ROWS
    for hb in range(aw // LANE):
        sl = slice(hb * LANE, (hb + 1) * LANE)
        q = rope(proj[:, sl]) * scale
        qT_ref[0, 0, sl, :] = q.T.astype(bf16)
        k = rope(proj[:, aw + hb * LANE:aw + (hb + 1) * LANE])
        k_ref[0, :, sl] = k.astype(bf16)
        v = proj[:, 2 * aw + hb * LANE:2 * aw + (hb + 1) * LANE]
        vT_ref[0, 0, hb * vrows:hb * vrows + LANE, :] = v.T.astype(bf16)
        vT_ref[0, 0, hb * vrows + LANE:(hb + 1) * vrows, :] = jnp.ones((ONES_ROWS, tm), bf16)
    cx = proj[:, 3 * aw:3 * aw + cw]
    cb = proj[:, 3 * aw + cw:3 * aw + 2 * cw]
    cc = proj[:, 3 * aw + 2 * cw:]
    u_ref[0] = (cc * cx).astype(bf16)
    cb_ref[0] = cb.astype(bf16)


def _attn_kernel(qT_ref, k_ref, vT_ref, lq1_ref, lk1_ref, lq2_ref, lk2_ref, sg_ref, o_ref,
                 acc_ref, m_ref, *, seq, lpp, hd):
    tq = qT_ref.shape[3]
    tk = k_ref.shape[1] // vT_ref.shape[1]
    n_chunks = vT_ref.shape[1]
    qt = qT_ref[0, 0]
    row = lax.broadcasted_iota(i32, qt.shape, 0)
    zero = jnp.zeros_like(qt)
    qq = jnp.concatenate([jnp.where(row < hd, qt, zero), jnp.where(row >= hd, qt, zero)], axis=1)

    m_ref[...] = jnp.full(m_ref.shape, MASK_VALUE, f32)
    acc_ref[...] = jnp.zeros(acc_ref.shape, f32)

    def chunk(c, masked):
        ks = c * tk if isinstance(c, int) else pl.multiple_of(c * tk, tk)
        kb = k_ref[0, pl.ds(ks, tk), :]
        s = jnp.dot(kb, qq, preferred_element_type=f32)
        if masked:
            krow = ks + lax.broadcasted_iota(i32, (tk, 1), 0)
            s = jnp.where((krow < seq) | (krow >= lpp - N_META), s, MASK_VALUE)
        m_old = m_ref[...]
        m_new = jnp.maximum(m_old, jnp.max(s, axis=0, keepdims=True))
        p = jnp.exp(s - m_new).astype(bf16)
        pv = jnp.dot(vT_ref[0, c], p, preferred_element_type=f32)
        acc_ref[...] = jnp.exp(m_old - m_new) * acc_ref[...] + pv
        m_ref[...] = m_new

    first_masked = seq // tk

    def body(c, carry):
        chunk(c, False)
        return carry

    lax.fori_loop(0, first_masked, body, 0)
    for c in range(first_masked, n_chunks):
        chunk(c, True)

    lam = (jnp.exp(jnp.sum(lq1_ref[...] * lk1_ref[...], axis=-1, keepdims=True))
           - jnp.exp(jnp.sum(lq2_ref[...] * lk2_ref[...], axis=-1, keepdims=True)) + LAM_INIT)
    acc = acc_ref[...]
    o = acc[:2 * hd, :] / acc[2 * hd:2 * hd + 1, :]
    d = (o[:, :tq] - lam * o[:, tq:]).T
    o_ref[0] = (_rms(d, sg_ref[...]) * (1.0 - LAM_INIT)).astype(bf16)


def _outproj_kernel(attn_ref, u_ref, up_ref, un_ref, cb_ref, cw_ref, h_ref, wo_ref, g2_ref, wrT_ref,
                    h1_ref, hn2t_ref, affT_ref):
    tm = u_ref.shape[1]
    d_model = h_ref.shape[2]
    u = u_ref[0].astype(f32)
    prev_row = up_ref[0][BF16_ROWS - 1:BF16_ROWS, :].astype(f32)
    next_row = un_ref[0][0:1, :].astype(f32)
    rid = lax.broadcasted_iota(i32, (tm, 1), 0)
    um1 = jnp.where(rid == 0, prev_row, pltpu.roll(u, 1, 0))
    up1 = jnp.where(rid == tm - 1, next_row, pltpu.roll(u, tm - 1, 0))
    w = cw_ref[...]
    conv = cb_ref[0].astype(f32) * (w[0:1] * um1 + w[1:2] * u + w[2:3] * up1)
    mixed = jnp.concatenate([attn_ref[0], conv.astype(bf16)], axis=1)
    h1 = h_ref[0] + jnp.dot(mixed, wo_ref[...], preferred_element_type=f32)
    h1_ref[0] = h1
    hn2 = _rms(h1, g2_ref[...])
    for c in range(d_model // LANE):
        hn2t_ref[0, pl.ds(c, tm, stride=SUBLANE), :] = hn2[:, c * LANE:(c + 1) * LANE]
    logits = lax.dot_general(wrT_ref[...], hn2.astype(bf16), (((1,), (1,)), ((), ())),
                             preferred_element_type=f32)
    p = jnp.exp(logits - jnp.max(logits, axis=0, keepdims=True))
    affT_ref[0] = p / jnp.sum(p, axis=0, keepdims=True)


def _route_kernel(affT_ref, idxT_ref, cum_ref, *, seq, lpp, cap):
    n_exp = affT_ref.shape[1]
    cpad = idxT_ref.shape[1]
    blk = 2 * LANE
    tok = lax.broadcasted_iota(i32, (n_exp, lpp), 1)
    a = jnp.where((tok < seq) | (tok >= lpp - N_META), affT_ref[0], -1.0)

    def count(mask):
        return jnp.sum(mask.astype(f32), axis=1, keepdims=True)

    def bisect(i, ans):
        cand = ans | jnp.left_shift(jnp.int32(1), 30 - i)
        keep = count(a >= lax.bitcast_convert_type(cand, f32)) >= cap
        return jnp.where(keep, cand, ans)

    tau = lax.bitcast_convert_type(lax.fori_loop(0, 31, bisect, jnp.zeros((n_exp, 1), i32)), f32)
    gt = a > tau
    eq = a == tau
    need = cap - count(gt)

    tri = (lax.broadcasted_iota(i32, (blk, blk), 0) <= lax.broadcasted_iota(i32, (blk, blk), 1)).astype(bf16)

    def cumsum_into(mask):
        carry = jnp.zeros((n_exp, 1), f32)
        ones = mask.astype(f32)
        for j in range(lpp // blk):
            c = jnp.dot(ones[:, j * blk:(j + 1) * blk].astype(bf16), tri, preferred_element_type=f32) + carry
            cum_ref[:, j * blk:(j + 1) * blk] = c
            carry = c[:, blk - 1:blk]

    cumsum_into(eq)
    sel = gt | (eq & (cum_ref[...] <= need))
    cumsum_into(sel)

    lane = lax.broadcasted_iota(i32, (SUBLANE, n_exp), 1)

    def rank_group(g, carry):
        r0 = pl.multiple_of(g * SUBLANE, SUBLANE)
        rr = (r0 + lax.broadcasted_iota(i32, (SUBLANE, 1), 0)).astype(f32)
        out = jnp.zeros((SUBLANE, n_exp), f32)
        for e in range(n_exp):
            cnt = jnp.sum((cum_ref[e:e + 1, :] <= rr).astype(f32), axis=1, keepdims=True)
            out = jnp.where(lane == e, cnt, out)
        idxT_ref[0, pl.ds(r0, SUBLANE), :] = jnp.where(rr < cap, out, 0.0).astype(i32)
        return carry

    lax.fori_loop(0, cpad // SUBLANE, rank_group, 0)


def _gather_kernel(idx_ref, src_ref, out_ref):
    cpad = idx_ref.shape[2]

    def body(j, carry):
        for i in range(GATHER_UNROLL):
            r = j * GATHER_UNROLL + i
            t = idx_ref[0, 0, r]
            out_ref[0, 0, pl.ds(pl.multiple_of(r * SUBLANE, SUBLANE), SUBLANE), :] = (
                src_ref[0, pl.ds(pl.multiple_of(t * SUBLANE, SUBLANE), SUBLANE), :])
        return carry

    lax.fori_loop(0, cpad // GATHER_UNROLL, body, 0)


def _ffn_kernel(xg_ref, wg_ref, wu_ref, wd_ref, wr_ref, y_ref, *, row_tile):
    e = pl.program_id(0)
    cpad = xg_ref.shape[2] // SUBLANE
    d_model = wg_ref.shape[1]
    n_exp = wr_ref.shape[1]
    nc = d_model // LANE
    for rt in range(cpad // row_tile):
        base = rt * row_tile * SUBLANE
        x = jnp.concatenate([xg_ref[0, 0, pl.ds(base + c, row_tile, stride=SUBLANE), :] for c in range(nc)],
                            axis=1).astype(bf16)
        a = jnp.dot(x, wg_ref[0], preferred_element_type=f32)
        u = jnp.dot(x, wu_ref[0], preferred_element_type=f32)
        act = (a * jax.nn.sigmoid(a) * u).astype(bf16)
        y = jnp.dot(act, wd_ref[0], preferred_element_type=f32)
        lg = jnp.dot(x, wr_ref[...], preferred_element_type=f32)
        p = jnp.exp(lg - jnp.max(lg, axis=-1, keepdims=True))
        lane = lax.broadcasted_iota(i32, (row_tile, n_exp), 1)
        gate = jnp.sum(jnp.where(lane == e, p, 0.0), axis=-1, keepdims=True) / jnp.sum(p, axis=-1, keepdims=True)
        y = y * gate
        for c in range(nc):
            y_ref[0, 0, pl.ds(base + c, row_tile, stride=SUBLANE), :] = y[:, c * LANE:(c + 1) * LANE]


def _combine_kernel(idx_ref, y_ref, h1_ref, gf_ref, out_ref, acc_ref, *, cap, n_exp):
    s = pl.program_id(1)
    tt = out_ref.shape[1]
    d_model = out_ref.shape[2]
    zrows = 64 * SUBLANE

    @pl.when(s == 0)
    def _():
        def zero(i, carry):
            acc_ref[pl.ds(pl.multiple_of(i * zrows, zrows), zrows), :] = jnp.zeros((zrows, LANE), f32)
            return carry
        lax.fori_loop(0, acc_ref.shape[0] // zrows, zero, 0)

    def add_rows(ranks):
        offs = [pl.multiple_of(idx_ref[0, 0, r] * SUBLANE, SUBLANE) for r in ranks]
        vals = [acc_ref[pl.ds(o, SUBLANE), :]
                + y_ref[0, 0, pl.ds(r * SUBLANE if isinstance(r, int) else pl.multiple_of(r * SUBLANE, SUBLANE),
                                    SUBLANE), :]
                for o, r in zip(offs, ranks)]
        for o, v in zip(offs, vals):
            acc_ref[pl.ds(o, SUBLANE), :] = v

    @pl.when(s < n_exp)
    def _():
        def body(j, carry):
            add_rows([j * GATHER_UNROLL + i for i in range(GATHER_UNROLL)])
            return carry
        full = cap // GATHER_UNROLL
        lax.fori_loop(0, full, body, 0)
        if cap % GATHER_UNROLL:
            add_rows(list(range(full * GATHER_UNROLL, cap)))

    @pl.when(s >= n_exp)
    def _():
        base = pl.multiple_of((s - n_exp) * (tt * SUBLANE), tt * SUBLANE)
        moe = jnp.concatenate([acc_ref[pl.ds(base + c, tt, stride=SUBLANE), :] for c in range(d_model // LANE)],
                              axis=1)
        out_ref[0] = _rms(h1_ref[0] + moe, gf_ref[...])


def kernel(x, meta_tokens, mix_norm_g, w_in, conv_w, lambda_q1, lambda_k1, lambda_q2, lambda_k2, attn_subln_g,
           w_out, ffn_norm_g, w_router, w_gate, w_up, w_down, final_norm_g):
    batch, seq, d_model = x.shape
    assert w_in.shape[0] == 1, "single-layer block"
    hd = lambda_q1.shape[-1]
    cw = conv_w.shape[-1]
    aw = d_model - cw
    n_heads = aw // (2 * hd)
    n_exp = w_router.shape[-1]
    d_ff = w_gate.shape[-1]
    assert 2 * hd == LANE and aw % LANE == 0 and cw % LANE == 0 and d_model == SUBLANE * LANE
    assert conv_w.shape[1] == 3
    length = N_META + seq
    cap = EC_CAPACITY_FACTOR * length // n_exp
    cpad = _round_up(cap, 3 * SUBLANE)
    tm = ROW_TILE
    lpp = _round_up(length + 1, tm)
    nt = lpp // tm
    tt = next(t for t in (512, 256, 128) if seq % t == 0)
    vrows = LANE + ONES_ROWS

    h = jnp.concatenate([x, jnp.zeros((batch, lpp - length, d_model), x.dtype),
                         jnp.broadcast_to(meta_tokens.astype(x.dtype)[None], (batch, N_META, d_model))], axis=1)
    pos = ((jnp.arange(lpp) + N_META) % lpp).astype(f32)
    inv_freq = ROPE_THETA ** (-jnp.arange(0, hd, 2, dtype=f32) / hd)
    ang = pos[:, None] * inv_freq[None, :]
    cos_t = jnp.tile(jnp.cos(ang), (1, LANE // (hd // 2)))
    sin_t = jnp.tile(jnp.concatenate([-jnp.sin(ang), jnp.sin(ang)], axis=-1), (1, LANE // hd))

    row2 = lambda v: v.reshape(1, -1).astype(f32)

    qT, k, vT, u, cb = pl.pallas_call(
        functools.partial(_inproj_kernel, aw=aw, cw=cw, hd=hd),
        grid=(batch, nt),
        in_specs=[pl.BlockSpec((1, tm, d_model), lambda b, i: (b, i, 0)),
                  pl.BlockSpec((1, d_model), lambda b, i: (0, 0)),
                  pl.BlockSpec((d_model, 3 * aw + 3 * cw), lambda b, i: (0, 0)),
                  pl.BlockSpec((tm, LANE), lambda b, i: (i, 0)),
                  pl.BlockSpec((tm, LANE), lambda b, i: (i, 0))],
        out_specs=[pl.BlockSpec((1, 1, aw, tm), lambda b, i: (b, i, 0, 0)),
                   pl.BlockSpec((1, tm, aw), lambda b, i: (b, i, 0)),
                   pl.BlockSpec((1, 1, n_heads * vrows, tm), lambda b, i: (b, i, 0, 0)),
                   pl.BlockSpec((1, tm, cw), lambda b, i: (b, i, 0)),
                   pl.BlockSpec((1, tm, cw), lambda b, i: (b, i, 0))],
        out_shape=[jax.ShapeDtypeStruct((batch, nt, aw, tm), bf16),
                   jax.ShapeDtypeStruct((batch, lpp, aw), bf16),
                   jax.ShapeDtypeStruct((batch, nt, n_heads * vrows, tm), bf16),
                   jax.ShapeDtypeStruct((batch, lpp, cw), bf16),
                   jax.ShapeDtypeStruct((batch, lpp, cw), bf16)],
        compiler_params=_params("parallel", "parallel"),
        name="inproj",
    )(h, row2(mix_norm_g[0]), w_in[0].astype(bf16), cos_t, sin_t)

    lam_spec = pl.BlockSpec((1, hd), lambda b, hh, i: (0, 0))
    attn = pl.pallas_call(
        functools.partial(_attn_kernel, seq=seq, lpp=lpp, hd=hd),
        grid=(batch, n_heads, nt),
        in_specs=[pl.BlockSpec((1, 1, LANE, tm), lambda b, hh, i: (b, i, hh, 0)),
                  pl.BlockSpec((1, lpp, LANE), lambda b, hh, i: (b, 0, hh)),
                  pl.BlockSpec((1, nt, vrows, tm), lambda b, hh, i: (b, 0, hh, 0)),
                  lam_spec, lam_spec, lam_spec, lam_spec,
                  pl.BlockSpec((1, LANE), lambda b, hh, i: (0, 0))],
        out_specs=pl.BlockSpec((1, tm, LANE), lambda b, hh, i: (b, i, hh)),
        out_shape=jax.ShapeDtypeStruct((batch, lpp, aw), bf16),
        scratch_shapes=[pltpu.VMEM((vrows, 2 * tm), f32), pltpu.VMEM((1, 2 * tm), f32)],
        compiler_params=_params("parallel", "parallel", "parallel"),
        name="diff_attn",
    )(qT, k, vT, row2(lambda_q1[0]), row2(lambda_k1[0]), row2(lambda_q2[0]), row2(lambda_k2[0]),
      row2(attn_subln_g[0]))

    halo = tm // BF16_ROWS
    n_halo = lpp // BF16_ROWS
    h1, hn2t, affT = pl.pallas_call(
        _outproj_kernel,
        grid=(batch, nt),
        in_specs=[pl.BlockSpec((1, tm, aw), lambda b, i: (b, i, 0)),
                  pl.BlockSpec((1, tm, cw), lambda b, i: (b, i, 0)),
                  pl.BlockSpec((1, BF16_ROWS, cw), lambda b, i: (b, (i * halo + n_halo - 1) % n_halo, 0)),
                  pl.BlockSpec((1, BF16_ROWS, cw), lambda b, i: (b, ((i + 1) * halo) % n_halo, 0)),
                  pl.BlockSpec((1, tm, cw), lambda b, i: (b, i, 0)),
                  pl.BlockSpec((3, cw), lambda b, i: (0, 0)),
                  pl.BlockSpec((1, tm, d_model), lambda b, i: (b, i, 0)),
                  pl.BlockSpec((aw + cw, d_model), lambda b, i: (0, 0)),
                  pl.BlockSpec((1, d_model), lambda b, i: (0, 0)),
                  pl.BlockSpec((n_exp, d_model), lambda b, i: (0, 0))],
        out_specs=[pl.BlockSpec((1, tm, d_model), lambda b, i: (b, i, 0)),
                   pl.BlockSpec((1, tm * SUBLANE, LANE), lambda b, i: (b, i, 0)),
                   pl.BlockSpec((1, n_exp, tm), lambda b, i: (b, 0, i))],
        out_shape=[jax.ShapeDtypeStruct((batch, lpp, d_model), f32),
                   jax.ShapeDtypeStruct((batch, lpp * SUBLANE, LANE), f32),
                   jax.ShapeDtypeStruct((batch, n_exp, lpp), f32)],
        compiler_params=_params("parallel", "parallel"),
        name="outproj_router",
    )(attn, u, u, u, cb, conv_w[0].astype(f32), h, w_out[0].astype(bf16), row2(ffn_norm_g[0]),
      w_router[0].T.astype(bf16))

    idxT = pl.pallas_call(
        functools.partial(_route_kernel, seq=seq, lpp=lpp, cap=cap),
        grid=(batch,),
        in_specs=[pl.BlockSpec((1, n_exp, lpp), lambda b: (b, 0, 0))],
        out_specs=pl.BlockSpec((1, cpad, n_exp), lambda b: (b, 0, 0)),
        out_shape=jax.ShapeDtypeStruct((batch, cpad, n_exp), i32),
        scratch_shapes=[pltpu.VMEM((n_exp, lpp), f32)],
        compiler_params=_params("parallel"),
        name="expert_choice",
    )(affT)
    idx = jnp.swapaxes(idxT, 1, 2).reshape(batch * n_exp, 1, cpad)

    idx_spec = lambda im: pl.BlockSpec((1, 1, cpad), im, memory_space=pltpu.SMEM)
    xg = pl.pallas_call(
        _gather_kernel,
        grid=(batch, n_exp),
        in_specs=[idx_spec(lambda b, e: (b * n_exp + e, 0, 0)),
                  pl.BlockSpec((1, lpp * SUBLANE, LANE), lambda b, e: (b, 0, 0), pipeline_mode=pl.Buffered(1))],
        out_specs=pl.BlockSpec((1, 1, cpad * SUBLANE, LANE), lambda b, e: (b, e, 0, 0)),
        out_shape=jax.ShapeDtypeStruct((batch, n_exp, cpad * SUBLANE, LANE), f32),
        compiler_params=_params("parallel", "arbitrary"),
        name="token_gather",
    )(idx, hn2t)

    y = pl.pallas_call(
        functools.partial(_ffn_kernel, row_tile=cpad // 3),
        grid=(n_exp, batch),
        in_specs=[pl.BlockSpec((1, 1, cpad * SUBLANE, LANE), lambda e, b: (b, e, 0, 0)),
                  pl.BlockSpec((1, d_model, d_ff), lambda e, b: (e, 0, 0)),
                  pl.BlockSpec((1, d_model, d_ff), lambda e, b: (e, 0, 0)),
                  pl.BlockSpec((1, d_ff, d_model), lambda e, b: (e, 0, 0)),
                  pl.BlockSpec((d_model, n_exp), lambda e, b: (0, 0))],
        out_specs=pl.BlockSpec((1, 1, cpad * SUBLANE, LANE), lambda e, b: (b, e, 0, 0)),
        out_shape=jax.ShapeDtypeStruct((batch, n_exp, cpad * SUBLANE, LANE), f32),
        compiler_params=_params("parallel", "parallel"),
        name="expert_ffn",
    )(xg, w_gate[0].astype(bf16), w_up[0].astype(bf16), w_down[0].astype(bf16), w_router[0].astype(bf16))

    n_out = seq // tt
    last = n_exp - 1
    out = pl.pallas_call(
        functools.partial(_combine_kernel, cap=cap, n_exp=n_exp),
        grid=(batch, n_exp + n_out),
        in_specs=[idx_spec(lambda b, s: (b * n_exp + jnp.minimum(s, last), 0, 0)),
                  pl.BlockSpec((1, 1, cpad * SUBLANE, LANE), lambda b, s: (b, jnp.minimum(s, last), 0, 0)),
                  pl.BlockSpec((1, tt, d_model), lambda b, s: (b, jnp.maximum(s - n_exp, 0), 0)),
                  pl.BlockSpec((1, d_model), lambda b, s: (0, 0))],
        out_specs=pl.BlockSpec((1, tt, d_model), lambda b, s: (b, jnp.maximum(s - n_exp, 0), 0)),
        out_shape=jax.ShapeDtypeStruct((batch, seq, d_model), x.dtype),
        scratch_shapes=[pltpu.VMEM((lpp * SUBLANE, LANE), f32)],
        compiler_params=_params("parallel", "arbitrary"),
        name="combine_norm",
    )(idx, y, h1, row2(final_norm_g))
    return out
```

```python
import functools

import jax
import jax.numpy as jnp
from jax import lax
from jax.experimental import pallas as pl
from jax.experimental.pallas import tpu as pltpu

N_META = 16
EC_CAPACITY_FACTOR = 2
ROPE_THETA = 10000.0
NORM_EPS = 1e-6
LAM_INIT = 0.8 - 0.6 * 1.0

LANE = 128
SUBLANE = 8
BF16_ROWS = 16
VMEM_LIMIT = 56 * 1024 * 1024

ROW_TILE = 768
ONES_ROWS = BF16_ROWS
MASK_VALUE = -1e30
GATHER_UNROLL = 8
ATTN_COL_TILE = 256
LOG2_E = 1.4426950408889634
EXP2_SAFE_JUMP = 100.0

f32 = jnp.float32
bf16 = jnp.bfloat16
i32 = jnp.int32


def _round_up(a, m):
    return -(-a // m) * m


def _params(*sem):
    return pltpu.CompilerParams(dimension_semantics=sem, vmem_limit_bytes=VMEM_LIMIT)


def _rms(x, g):
    return x * lax.rsqrt(jnp.mean(x * x, axis=-1, keepdims=True) + NORM_EPS) * g


def _residual_tile(x_ref, tail_ref, n_full):
    return jnp.where(pl.program_id(1) < n_full, x_ref[0], tail_ref[0])


def _inproj_kernel(x_ref, tail_ref, g_ref, w_ref, cos_ref, sin_ref, qT_ref, k_ref, vT_ref, u_ref, cb_ref,
                   *, aw, cw, hd, n_full):
    tm = x_ref.shape[1]
    hn = _rms(_residual_tile(x_ref, tail_ref, n_full), g_ref[...]).astype(bf16)
    proj = jnp.dot(hn, w_ref[...], preferred_element_type=f32)
    cos = cos_ref[...]
    sin = sin_ref[...]
    lane = lax.broadcasted_iota(i32, (tm, LANE), 1)
    first_half = (lane % hd) < (hd // 2)

    def rope(x):
        rot = jnp.where(first_half, pltpu.roll(x, LANE - hd // 2, 1), pltpu.roll(x, hd // 2, 1))
        return x * cos + rot * sin

    scale = hd ** -0.5 * LOG2_E
    vrows = LANE + ONES_ROWS
    for hb in range(aw // LANE):
        sl = slice(hb * LANE, (hb + 1) * LANE)
        q = rope(proj[:, sl]) * scale
        qT_ref[0, 0, sl, :] = q.T.astype(bf16)
        k = rope(proj[:, aw + hb * LANE:aw + (hb + 1) * LANE])
        k_ref[0, :, sl] = k.astype(bf16)
        v = proj[:, 2 * aw + hb * LANE:2 * aw + (hb + 1) * LANE]
        vT_ref[0, 0, hb * vrows:hb * vrows + LANE, :] = v.T.astype(bf16)
        vT_ref[0, 0, hb * vrows + LANE:(hb + 1) * vrows, :] = jnp.ones((ONES_ROWS, tm), bf16)
    cx = proj[:, 3 * aw:3 * aw + cw]
    cb = proj[:, 3 * aw + cw:3 * aw + 2 * cw]
    cc = proj[:, 3 * aw + 2 * cw:]
    u_ref[0] = (cc * cx).astype(bf16)
    cb_ref[0] = cb.astype(bf16)


def _attn_kernel(qT_ref, k_ref, vT_ref, lq1_ref, lk1_ref, lq2_ref, lk2_ref, sg_ref, o_ref,
                 qq_ref, pa_ref, pb_ref, acc_ref, m_ref, cm_ref, alpha_ref, jump_ref, *, seq, lpp, hd):
    tq = qT_ref.shape[3]
    n_chunks = vT_ref.shape[1]
    tk = k_ref.shape[1] // n_chunks
    col_tiles = [slice(j, j + ATTN_COL_TILE) for j in range(0, 2 * tq, ATTN_COL_TILE)]
    qt = qT_ref[0, 0]
    row = lax.broadcasted_iota(i32, qt.shape, 0)
    zero = jnp.zeros_like(qt)
    qq_ref[:, :tq] = jnp.where(row < hd, qt, zero)
    qq_ref[:, tq:] = jnp.where(row >= hd, qt, zero)

    def key_start(c):
        return c * tk if isinstance(c, int) else pl.multiple_of(c * tk, tk)

    def masked_scores(c, cs, masked):
        s = jnp.dot(k_ref[0, pl.ds(key_start(c), tk), :], qq_ref[:, cs], preferred_element_type=f32)
        if masked:
            krow = key_start(c) + lax.broadcasted_iota(i32, (tk, 1), 0)
            s = jnp.where((krow < seq) | (krow >= lpp - N_META), s, MASK_VALUE)
        return s

    bufs = (pa_ref, pb_ref)

    def probs(c, parity, masked):
        for cs in col_tiles:
            s = masked_scores(c, cs, masked)
            bufs[parity][:, cs] = jnp.exp2(s - m_ref[:, cs]).astype(bf16)
            cm_ref[:, cs] = jnp.max(s, axis=0, keepdims=True)
        m_old = m_ref[...]
        m_new = jnp.maximum(m_old, cm_ref[...])
        jump_ref[...] = jnp.maximum(jump_ref[...], cm_ref[...] - m_old)
        alpha_ref[1 - parity] = jnp.exp2(m_old - m_new)
        m_ref[...] = m_new

    def weighted_values(c, parity):
        for cs in col_tiles:
            acc_ref[:, cs] = (alpha_ref[parity, :, cs] * acc_ref[:, cs]
                              + jnp.dot(vT_ref[0, c], bufs[parity][:, cs], preferred_element_type=f32))

    first_masked = seq // tk
    n_pairs = (first_masked - 1) // 2
    acc_ref[...] = jnp.zeros(acc_ref.shape, f32)
    alpha_ref[...] = jnp.ones(alpha_ref.shape, f32)
    jump_ref[...] = jnp.zeros(jump_ref.shape, f32)
    m_ref[...] = jnp.max(jnp.dot(k_ref[0, 0:BF16_ROWS, :], qq_ref[...], preferred_element_type=f32),
                         axis=0, keepdims=True)
    probs(0, 0, first_masked == 0)

    def pair(i, carry):
        for half in range(2):
            weighted_values(2 * i + half, half)
            probs(2 * i + half + 1, 1 - half, False)
        return carry

    lax.fori_loop(0, n_pairs, pair, 0)
    for c in range(2 * n_pairs, n_chunks):
        weighted_values(c, c % 2)
        if c + 1 < n_chunks:
            probs(c + 1, (c + 1) % 2, c + 1 >= first_masked)

    @pl.when(jnp.max(jump_ref[...]) > EXP2_SAFE_JUMP)
    def _():
        acc_ref[...] = jnp.zeros(acc_ref.shape, f32)
        m_ref[...] = jnp.full(m_ref.shape, MASK_VALUE, f32)

        def exact_chunk(c, carry):
            ks = pl.multiple_of(c * tk, tk)
            krow = ks + lax.broadcasted_iota(i32, (tk, 1), 0)
            valid = (krow < seq) | (krow >= lpp - N_META)
            for cs in col_tiles:
                s = jnp.where(valid, masked_scores(c, cs, False), MASK_VALUE)
                m_old = m_ref[:, cs]
                m_new = jnp.maximum(m_old, jnp.max(s, axis=0, keepdims=True))
                p = jnp.exp2(s - m_new).astype(bf16)
                acc_ref[:, cs] = (jnp.exp2(m_old - m_new) * acc_ref[:, cs]
                                  + jnp.dot(vT_ref[0, c], p, preferred_element_type=f32))
                m_ref[:, cs] = m_new
            return carry

        lax.fori_loop(0, n_chunks, exact_chunk, 0)

    lam = (jnp.exp(jnp.sum(lq1_ref[...] * lk1_ref[...], axis=-1, keepdims=True))
           - jnp.exp(jnp.sum(lq2_ref[...] * lk2_ref[...], axis=-1, keepdims=True)) + LAM_INIT)
    acc = acc_ref[...]
    o = acc[:2 * hd, :] / acc[2 * hd:2 * hd + 1, :]
    d = (o[:, :tq] - lam * o[:, tq:]).T
    o_ref[0] = (_rms(d, sg_ref[...]) * (1.0 - LAM_INIT)).astype(bf16)


def _outproj_kernel(attn_ref, u_ref, up_ref, un_ref, cb_ref, cw_ref, x_ref, tail_ref, wo_ref, g2_ref, wrT_ref,
                    h1_ref, hn2t_ref, affT_ref, *, n_full):
    tm = u_ref.shape[1]
    d_model = x_ref.shape[2]
    u = u_ref[0].astype(f32)
    prev_row = up_ref[0][BF16_ROWS - 1:BF16_ROWS, :].astype(f32)
    next_row = un_ref[0][0:1, :].astype(f32)
    rid = lax.broadcasted_iota(i32, (tm, 1), 0)
    um1 = jnp.where(rid == 0, prev_row, pltpu.roll(u, 1, 0))
    up1 = jnp.where(rid == tm - 1, next_row, pltpu.roll(u, tm - 1, 0))
    w = cw_ref[...]
    conv = cb_ref[0].astype(f32) * (w[0:1] * um1 + w[1:2] * u + w[2:3] * up1)
    mixed = jnp.concatenate([attn_ref[0], conv.astype(bf16)], axis=1)
    h1 = _residual_tile(x_ref, tail_ref, n_full) + jnp.dot(mixed, wo_ref[...], preferred_element_type=f32)
    h1_ref[0] = h1
    hn2 = _rms(h1, g2_ref[...])
    for c in range(d_model // LANE):
        hn2t_ref[0, pl.ds(c, tm, stride=SUBLANE), :] = hn2[:, c * LANE:(c + 1) * LANE]
    logits = lax.dot_general(wrT_ref[...], hn2.astype(bf16), (((1,), (1,)), ((), ())),
                             preferred_element_type=f32)
    p = jnp.exp(logits - jnp.max(logits, axis=0, keepdims=True))
    affT_ref[0] = p / jnp.sum(p, axis=0, keepdims=True)


def _route_kernel(aff_ref, idxT_ref, *, seq, lpp, cap):
    n_exp, ntl = aff_ref.shape[1], aff_ref.shape[2]
    cpad = idxT_ref.shape[1]
    tok = (lax.broadcasted_iota(i32, (ntl, LANE), 0) * LANE + lax.broadcasted_iota(i32, (ntl, LANE), 1))
    valid = (tok < seq) | ((tok >= lpp - N_META) & (tok < lpp))
    a = jnp.where(valid[None], aff_ref[0], -1.0)

    def count(mask):
        return jnp.sum(jnp.sum(mask.astype(f32), axis=1, keepdims=True), axis=2, keepdims=True)

    def bisect(i, ans):
        cand = ans | jnp.left_shift(jnp.int32(1), 30 - i)
        keep = count(a >= lax.bitcast_convert_type(cand, f32)) >= cap
        return jnp.where(keep, cand, ans)

    tau = lax.bitcast_convert_type(lax.fori_loop(0, 31, bisect, jnp.zeros((n_exp, 1, 1), i32)), f32)
    gt = (a > tau).astype(f32)
    eq = (a == tau).astype(f32)
    need = cap - count(a > tau)

    sq0 = lax.broadcasted_iota(i32, (LANE, LANE), 0)
    sq1 = lax.broadcasted_iota(i32, (LANE, LANE), 1)
    tri = (sq0 <= sq1).astype(bf16)
    pick_last = (sq0 == LANE - 1).astype(bf16)
    before = (lax.broadcasted_iota(i32, (ntl, ntl), 1) < lax.broadcasted_iota(i32, (ntl, ntl), 0)).astype(bf16)

    def prefix_count(m):
        local = jnp.dot(m.astype(bf16), tri, preferred_element_type=f32)
        tot = jnp.dot(local.astype(bf16), pick_last, preferred_element_type=f32)
        base = jnp.dot(before, tot.astype(bf16), preferred_element_type=f32)
        return local + base, base, tot

    rr = lax.broadcasted_iota(i32, (cpad, LANE), 0).astype(f32)
    tile_tok0 = (lax.broadcasted_iota(i32, (cpad, LANE), 1) * LANE).astype(f32)
    diag = lax.broadcasted_iota(i32, (ntl, LANE), 0) == lax.broadcasted_iota(i32, (ntl, LANE), 1)
    lane_e = lax.broadcasted_iota(i32, (cpad, n_exp), 1)
    zpad = jnp.zeros((LANE - ntl, LANE), f32)
    out = jnp.zeros((cpad, n_exp), f32)
    for e in range(n_exp):
        eq_cum, _, _ = prefix_count(eq[e])
        sel = jnp.maximum(gt[e], eq[e] * (eq_cum <= need[e]).astype(f32))
        cum, base, tot = prefix_count(sel)
        t_lo = jnp.sum(jnp.where(diag, base, 0.0), axis=0, keepdims=True)
        t_hi = jnp.sum(jnp.where(diag, base + tot, 0.0), axis=0, keepdims=True)
        in_tile = (t_lo <= rr) & (t_hi > rr)
        hi = jnp.floor(cum * (1.0 / 32.0))
        lo = cum - 32.0 * hi
        rhs = jnp.concatenate([jnp.concatenate([hi, zpad], axis=0), jnp.concatenate([lo, zpad], axis=0)], axis=1)
        g = jnp.dot(in_tile.astype(f32).astype(bf16), rhs.astype(bf16), preferred_element_type=f32)
        tile_cum = 32.0 * g[:, :LANE] + g[:, LANE:]
        pos = jnp.sum((tile_cum <= rr).astype(f32) + jnp.where(in_tile, tile_tok0, 0.0), axis=1, keepdims=True)
        out = jnp.where(lane_e == e, pos, out)
    rank = lax.broadcasted_iota(i32, (cpad, n_exp), 0)
    idxT_ref[0] = jnp.where(rank < cap, out, 0.0).astype(i32)


def _gather_kernel(idx_ref, src_ref, out_ref):
    cpad = idx_ref.shape[2]

    def body(j, carry):
        for i in range(GATHER_UNROLL):
            r = j * GATHER_UNROLL + i
            t = idx_ref[0, 0, r]
            out_ref[0, 0, pl.ds(pl.multiple_of(r * SUBLANE, SUBLANE), SUBLANE), :] = (
                src_ref[0, pl.ds(pl.multiple_of(t * SUBLANE, SUBLANE), SUBLANE), :])
        return carry

    lax.fori_loop(0, cpad // GATHER_UNROLL, body, 0)


def _ffn_kernel(xg_ref, wg_ref, wu_ref, wd_ref, wr_ref, y_ref, *, row_tile):
    e = pl.program_id(0)
    cpad = xg_ref.shape[2] // SUBLANE
    d_model = wg_ref.shape[1]
    n_exp = wr_ref.shape[1]
    nc = d_model // LANE
    for rt in range(cpad // row_tile):
        base = rt * row_tile * SUBLANE
        x = jnp.concatenate([xg_ref[0, 0, pl.ds(base + c, row_tile, stride=SUBLANE), :] for c in range(nc)],
                            axis=1).astype(bf16)
        a = jnp.dot(x, wg_ref[0], preferred_element_type=f32)
        u = jnp.dot(x, wu_ref[0], preferred_element_type=f32)
        act = (a * jax.nn.sigmoid(a) * u).astype(bf16)
        y = jnp.dot(act, wd_ref[0], preferred_element_type=f32)
        lg = jnp.dot(x, wr_ref[...], preferred_element_type=f32)
        p = jnp.exp(lg - jnp.max(lg, axis=-1, keepdims=True))
        lane = lax.broadcasted_iota(i32, (row_tile, n_exp), 1)
        gate = jnp.sum(jnp.where(lane == e, p, 0.0), axis=-1, keepdims=True) / jnp.sum(p, axis=-1, keepdims=True)
        y = y * gate
        for c in range(nc):
            y_ref[0, 0, pl.ds(base + c, row_tile, stride=SUBLANE), :] = y[:, c * LANE:(c + 1) * LANE]


def _combine_kernel(idx_ref, y_ref, h1_ref, gf_ref, out_ref, acc_ref, *, cap, n_exp):
    s = pl.program_id(1)
    tt = out_ref.shape[1]
    d_model = out_ref.shape[2]
    zrows = 64 * SUBLANE

    @pl.when(s == 0)
    def _():
        def zero(i, carry):
            acc_ref[pl.ds(pl.multiple_of(i * zrows, zrows), zrows), :] = jnp.zeros((zrows, LANE), f32)
            return carry
        lax.fori_loop(0, acc_ref.shape[0] // zrows, zero, 0)

    def add_rows(ranks):
        offs = [pl.multiple_of(idx_ref[0, 0, r] * SUBLANE, SUBLANE) for r in ranks]
        vals = [acc_ref[pl.ds(o, SUBLANE), :]
                + y_ref[0, 0, pl.ds(r * SUBLANE if isinstance(r, int) else pl.multiple_of(r * SUBLANE, SUBLANE),
                                    SUBLANE), :]
                for o, r in zip(offs, ranks)]
        for o, v in zip(offs, vals):
            acc_ref[pl.ds(o, SUBLANE), :] = v

    @pl.when(s < n_exp)
    def _():
        def body(j, carry):
            add_rows([j * GATHER_UNROLL + i for i in range(GATHER_UNROLL)])
            return carry
        full = cap // GATHER_UNROLL
        lax.fori_loop(0, full, body, 0)
        if cap % GATHER_UNROLL:
            add_rows(list(range(full * GATHER_UNROLL, cap)))

    @pl.when(s >= n_exp)
    def _():
        base = pl.multiple_of((s - n_exp) * (tt * SUBLANE), tt * SUBLANE)
        moe = jnp.concatenate([acc_ref[pl.ds(base + c, tt, stride=SUBLANE), :] for c in range(d_model // LANE)],
                              axis=1)
        out_ref[0] = _rms(h1_ref[0] + moe, gf_ref[...])


def kernel(x, meta_tokens, mix_norm_g, w_in, conv_w, lambda_q1, lambda_k1, lambda_q2, lambda_k2, attn_subln_g,
           w_out, ffn_norm_g, w_router, w_gate, w_up, w_down, final_norm_g):
    batch, seq, d_model = x.shape
    assert w_in.shape[0] == 1, "single-layer block"
    hd = lambda_q1.shape[-1]
    cw = conv_w.shape[-1]
    aw = d_model - cw
    n_heads = aw // (2 * hd)
    n_exp = w_router.shape[-1]
    d_ff = w_gate.shape[-1]
    assert 2 * hd == LANE and aw % LANE == 0 and cw % LANE == 0 and d_model == SUBLANE * LANE
    assert conv_w.shape[1] == 3
    length = N_META + seq
    cap = EC_CAPACITY_FACTOR * length // n_exp
    cpad = _round_up(cap, 3 * SUBLANE)
    tm = ROW_TILE
    lpp = _round_up(length + 1, tm)
    nt = lpp // tm
    tt = next(t for t in (512, 256, 128) if seq % t == 0)
    vrows = LANE + ONES_ROWS

    n_full = seq // tm
    assert n_full >= 1 and seq >= BF16_ROWS
    tail = jnp.concatenate([x[:, n_full * tm:], jnp.zeros((batch, lpp - length, d_model), x.dtype),
                            jnp.broadcast_to(meta_tokens.astype(x.dtype)[None], (batch, N_META, d_model))], axis=1)
    x_spec = pl.BlockSpec((1, tm, d_model), lambda b, i: (b, jnp.minimum(i, n_full - 1), 0))
    tail_spec = pl.BlockSpec((1, tm, d_model), lambda b, i: (b, jnp.maximum(i - n_full, 0), 0))
    pos =((jnp.arange(lpp) + N_META) % lpp).astype(f32)
    inv_freq = ROPE_THETA ** (-jnp.arange(0, hd, 2, dtype=f32) / hd)
    ang = pos[:, None] * inv_freq[None, :]
    cos_t = jnp.tile(jnp.cos(ang), (1, LANE // (hd // 2)))
    sin_t = jnp.tile(jnp.concatenate([-jnp.sin(ang), jnp.sin(ang)], axis=-1), (1, LANE // hd))

    row2 = lambda v: v.reshape(1, -1).astype(f32)

    qT, k, vT, u, cb = pl.pallas_call(
        functools.partial(_inproj_kernel, aw=aw, cw=cw, hd=hd, n_full=n_full),
        grid=(batch, nt),
        in_specs=[x_spec, tail_spec,
                  pl.BlockSpec((1, d_model), lambda b, i: (0, 0)),
                  pl.BlockSpec((d_model, 3 * aw + 3 * cw), lambda b, i: (0, 0)),
                  pl.BlockSpec((tm, LANE), lambda b, i: (i, 0)),
                  pl.BlockSpec((tm, LANE), lambda b, i: (i, 0))],
        out_specs=[pl.BlockSpec((1, 1, aw, tm), lambda b, i: (b, i, 0, 0)),
                   pl.BlockSpec((1, tm, aw), lambda b, i: (b, i, 0)),
                   pl.BlockSpec((1, 1, n_heads * vrows, tm), lambda b, i: (b, i, 0, 0)),
                   pl.BlockSpec((1, tm, cw), lambda b, i: (b, i, 0)),
                   pl.BlockSpec((1, tm, cw), lambda b, i: (b, i, 0))],
        out_shape=[jax.ShapeDtypeStruct((batch, nt, aw, tm), bf16),
                   jax.ShapeDtypeStruct((batch, lpp, aw), bf16),
                   jax.ShapeDtypeStruct((batch, nt, n_heads * vrows, tm), bf16),
                   jax.ShapeDtypeStruct((batch, lpp, cw), bf16),
                   jax.ShapeDtypeStruct((batch, lpp, cw), bf16)],
        compiler_params=_params("parallel", "parallel"),
        name="inproj",
    )(x, tail, row2(mix_norm_g[0]), w_in[0].astype(bf16), cos_t, sin_t)

    lam_spec = pl.BlockSpec((1, hd), lambda b, hh, i: (0, 0))
    attn = pl.pallas_call(
        functools.partial(_attn_kernel, seq=seq, lpp=lpp, hd=hd),
        grid=(batch, n_heads, nt),
        in_specs=[pl.BlockSpec((1, 1, LANE, tm), lambda b, hh, i: (b, i, hh, 0)),
                  pl.BlockSpec((1, lpp, LANE), lambda b, hh, i: (b, 0, hh)),
                  pl.BlockSpec((1, nt, vrows, tm), lambda b, hh, i: (b, 0, hh, 0)),
                  lam_spec, lam_spec, lam_spec, lam_spec,
                  pl.BlockSpec((1, LANE), lambda b, hh, i: (0, 0))],
        out_specs=pl.BlockSpec((1, tm, LANE), lambda b, hh, i: (b, i, hh)),
        out_shape=jax.ShapeDtypeStruct((batch, lpp, aw), bf16),
        scratch_shapes=[pltpu.VMEM((LANE, 2 * tm), bf16), pltpu.VMEM((tm, 2 * tm), bf16),
                        pltpu.VMEM((tm, 2 * tm), bf16), pltpu.VMEM((vrows, 2 * tm), f32),
                        pltpu.VMEM((1, 2 * tm), f32), pltpu.VMEM((1, 2 * tm), f32),
                        pltpu.VMEM((2, 1, 2 * tm), f32), pltpu.VMEM((1, 2 * tm), f32)],
        compiler_params=_params("parallel", "parallel", "parallel"),
        name="diff_attn",
    )(qT, k, vT, row2(lambda_q1[0]), row2(lambda_k1[0]), row2(lambda_q2[0]), row2(lambda_k2[0]),
      row2(attn_subln_g[0]))

    halo = tm // BF16_ROWS
    n_halo = lpp // BF16_ROWS
    h1, hn2t, affT = pl.pallas_call(
        functools.partial(_outproj_kernel, n_full=n_full),
        grid=(batch, nt),
        in_specs=[pl.BlockSpec((1, tm, aw), lambda b, i: (b, i, 0)),
                  pl.BlockSpec((1, tm, cw), lambda b, i: (b, i, 0)),
                  pl.BlockSpec((1, BF16_ROWS, cw), lambda b, i: (b, (i * halo + n_halo - 1) % n_halo, 0)),
                  pl.BlockSpec((1, BF16_ROWS, cw), lambda b, i: (b, ((i + 1) * halo) % n_halo, 0)),
                  pl.BlockSpec((1, tm, cw), lambda b, i: (b, i, 0)),
                  pl.BlockSpec((3, cw), lambda b, i: (0, 0)),
                  x_spec, tail_spec,
                  pl.BlockSpec((aw + cw, d_model), lambda b, i: (0, 0)),
                  pl.BlockSpec((1, d_model), lambda b, i: (0, 0)),
                  pl.BlockSpec((n_exp, d_model), lambda b, i: (0, 0))],
        out_specs=[pl.BlockSpec((1, tm, d_model), lambda b, i: (b, i, 0)),
                   pl.BlockSpec((1, tm * SUBLANE, LANE), lambda b, i: (b, i, 0)),
                   pl.BlockSpec((1, n_exp, tm), lambda b, i: (b, 0, i))],
        out_shape=[jax.ShapeDtypeStruct((batch, lpp, d_model), f32),
                   jax.ShapeDtypeStruct((batch, lpp * SUBLANE, LANE), f32),
                   jax.ShapeDtypeStruct((batch, n_exp, lpp), f32)],
        compiler_params=_params("parallel", "parallel"),
        name="outproj_router",
    )(attn, u, u, u, cb, conv_w[0].astype(f32), x, tail, w_out[0].astype(bf16), row2(ffn_norm_g[0]),
      w_router[0].T.astype(bf16))

    ntl = _round_up(lpp // LANE, SUBLANE)
    assert ntl <= LANE
    aff_tiles = jnp.pad(affT, ((0, 0), (0, 0), (0, ntl * LANE - lpp))).reshape(batch, n_exp, ntl, LANE)
    idxT = pl.pallas_call(
        functools.partial(_route_kernel, seq=seq, lpp=lpp, cap=cap),
        grid=(batch,),
        in_specs=[pl.BlockSpec((1, n_exp, ntl, LANE), lambda b: (b, 0, 0, 0))],
        out_specs=pl.BlockSpec((1, cpad, n_exp), lambda b: (b, 0, 0)),
        out_shape=jax.ShapeDtypeStruct((batch, cpad, n_exp), i32),
        compiler_params=_params("parallel"),
        name="expert_choice",
    )(aff_tiles)
    idx = jnp.swapaxes(idxT, 1, 2).reshape(batch * n_exp, 1, cpad)

    idx_spec = lambda im: pl.BlockSpec((1, 1, cpad), im, memory_space=pltpu.SMEM)
    xg = pl.pallas_call(
        _gather_kernel,
        grid=(batch, n_exp),
        in_specs=[idx_spec(lambda b, e: (b * n_exp + e, 0, 0)),
                  pl.BlockSpec((1, lpp * SUBLANE, LANE), lambda b, e: (b, 0, 0), pipeline_mode=pl.Buffered(1))],
        out_specs=pl.BlockSpec((1, 1, cpad * SUBLANE, LANE), lambda b, e: (b, e, 0, 0)),
        out_shape=jax.ShapeDtypeStruct((batch, n_exp, cpad * SUBLANE, LANE), f32),
        compiler_params=_params("parallel", "arbitrary"),
        name="token_gather",
    )(idx, hn2t)

    y = pl.pallas_call(
        functools.partial(_ffn_kernel, row_tile=cpad // 3),
        grid=(n_exp, batch),
        in_specs=[pl.BlockSpec((1, 1, cpad * SUBLANE, LANE), lambda e, b: (b, e, 0, 0)),
                  pl.BlockSpec((1, d_model, d_ff), lambda e, b: (e, 0, 0)),
                  pl.BlockSpec((1, d_model, d_ff), lambda e, b: (e, 0, 0)),
                  pl.BlockSpec((1, d_ff, d_model), lambda e, b: (e, 0, 0)),
                  pl.BlockSpec((d_model, n_exp), lambda e, b: (0, 0))],
        out_specs=pl.BlockSpec((1, 1, cpad * SUBLANE, LANE), lambda e, b: (b, e, 0, 0)),
        out_shape=jax.ShapeDtypeStruct((batch, n_exp, cpad * SUBLANE, LANE), f32),
        compiler_params=_params("parallel", "parallel"),
        name="expert_ffn",
    )(xg, w_gate[0].astype(bf16), w_up[0].astype(bf16), w_down[0].astype(bf16), w_router[0].astype(bf16))

    n_out = seq // tt
    last = n_exp - 1
    out = pl.pallas_call(
        functools.partial(_combine_kernel, cap=cap, n_exp=n_exp),
        grid=(batch, n_exp + n_out),
        in_specs=[idx_spec(lambda b, s: (b * n_exp + jnp.minimum(s, last), 0, 0)),
                  pl.BlockSpec((1, 1, cpad * SUBLANE, LANE), lambda b, s: (b, jnp.minimum(s, last), 0, 0)),
                  pl.BlockSpec((1, tt, d_model), lambda b, s: (b, jnp.maximum(s - n_exp, 0), 0)),
                  pl.BlockSpec((1, d_model), lambda b, s: (0, 0))],
        out_specs=pl.BlockSpec((1, tt, d_model), lambda b, s: (b, jnp.maximum(s - n_exp, 0), 0)),
        out_shape=jax.ShapeDtypeStruct((batch, seq, d_model), x.dtype),
        scratch_shapes=[pltpu.VMEM((lpp * SUBLANE, LANE), f32)],
        compiler_params=_params("parallel", "arbitrary"),
        name="combine_norm",
    )(idx, y, h1, row2(final_norm_g))
    return out
```

```python
import functools

import jax
import jax.numpy as jnp
from jax import lax
from jax.experimental import pallas as pl
from jax.experimental.pallas import tpu as pltpu

N_META = 16
EC_CAPACITY_FACTOR = 2
ROPE_THETA = 10000.0
NORM_EPS = 1e-6
LAM_INIT = 0.8 - 0.6 * 1.0

LANE = 128
SUBLANE = 8
BF16_ROWS = 16
VMEM_LIMIT = 56 * 1024 * 1024

ROW_TILE = 768
MASK_VALUE = -1e30
GATHER_UNROLL = 8
ONES_ROWS = BF16_ROWS
FFN_ROW_TILES = 3
ATTN_COL_TILE = 256
LOG2_E = 1.4426950408889634
EXP2_SAFE_JUMP = 100.0

f32 = jnp.float32
bf16 = jnp.bfloat16
i32 = jnp.int32


def _round_up(a, m):
    return -(-a // m) * m


def _params(*sem):
    return pltpu.CompilerParams(dimension_semantics=sem, vmem_limit_bytes=VMEM_LIMIT)


def _rms(x, g):
    return x * lax.rsqrt(jnp.mean(x * x, axis=-1, keepdims=True) + NORM_EPS) * g


def _residual_tile(x_ref, tail_ref, n_full):
    return jnp.where(pl.program_id(1) < n_full, x_ref[0], tail_ref[0])


def _inproj_kernel(x_ref, tail_ref, g_ref, w_ref, cos_ref, sin_ref, qT_ref, k_ref, vT_ref, u_ref, cb_ref,
                   *, aw, cw, hd, n_full):
    tm = x_ref.shape[1]
    hn = _rms(_residual_tile(x_ref, tail_ref, n_full), g_ref[...]).astype(bf16)
    proj = jnp.dot(hn, w_ref[...], preferred_element_type=f32)
    cos = cos_ref[...]
    sin = sin_ref[...]
    lane = lax.broadcasted_iota(i32, (tm, LANE), 1)
    first_half = (lane % hd) < (hd // 2)

    def rope(x):
        rot = jnp.where(first_half, pltpu.roll(x, LANE - hd // 2, 1), pltpu.roll(x, hd // 2, 1))
        return x * cos + rot * sin

    scale = hd ** -0.5 * LOG2_E
    vrows = LANE + ONES_ROWS
    for hb in range(aw // LANE):
        sl = slice(hb * LANE, (hb + 1) * LANE)
        q = rope(proj[:, sl]) * scale
        qT_ref[0, 0, sl, :] = q.T.astype(bf16)
        k = rope(proj[:, aw + hb * LANE:aw + (hb + 1) * LANE])
        k_ref[0, :, sl] = k.astype(bf16)
        v = proj[:, 2 * aw + hb * LANE:2 * aw + (hb + 1) * LANE]
        vT_ref[0, 0, hb * vrows:hb * vrows + LANE, :] = v.T.astype(bf16)
        vT_ref[0, 0, hb * vrows + LANE:(hb + 1) * vrows, :] = jnp.ones((ONES_ROWS, tm), bf16)
    cx = proj[:, 3 * aw:3 * aw + cw]
    cb = proj[:, 3 * aw + cw:3 * aw + 2 * cw]
    cc = proj[:, 3 * aw + 2 * cw:]
    u_ref[0] = (cc * cx).astype(bf16)
    cb_ref[0] = cb.astype(bf16)


def _attn_kernel(qT_ref, k_ref, vT_ref, lq1_ref, lk1_ref, lq2_ref, lk2_ref, sg_ref, o_ref,
                 qq_ref, pa_ref, pb_ref, acc_ref, m_ref, cm_ref, alpha_ref, jump_ref, *, seq, lpp, hd):
    tq = qT_ref.shape[3]
    n_chunks = vT_ref.shape[1]
    tk = k_ref.shape[1] // n_chunks
    col_tiles = [slice(j, j + ATTN_COL_TILE) for j in range(0, 2 * tq, ATTN_COL_TILE)]
    qt = qT_ref[0, 0]
    row = lax.broadcasted_iota(i32, qt.shape, 0)
    zero = jnp.zeros_like(qt)
    qq_ref[:, :tq] = jnp.where(row < hd, qt, zero)
    qq_ref[:, tq:] = jnp.where(row >= hd, qt, zero)

    def key_start(c):
        return c * tk if isinstance(c, int) else pl.multiple_of(c * tk, tk)

    def masked_scores(c, cs, masked):
        s = jnp.dot(k_ref[0, pl.ds(key_start(c), tk), :], qq_ref[:, cs], preferred_element_type=f32)
        if masked:
            krow = key_start(c) + lax.broadcasted_iota(i32, (tk, 1), 0)
            s = jnp.where((krow < seq) | (krow >= lpp - N_META), s, MASK_VALUE)
        return s

    bufs = (pa_ref, pb_ref)

    def probs(c, parity, masked):
        for cs in col_tiles:
            s = masked_scores(c, cs, masked)
            bufs[parity][:, cs] = jnp.exp2(s - m_ref[:, cs]).astype(bf16)
            cm_ref[:, cs] = jnp.max(s, axis=0, keepdims=True)
        m_old = m_ref[...]
        m_new = jnp.maximum(m_old, cm_ref[...])
        jump_ref[...] = jnp.maximum(jump_ref[...], cm_ref[...] - m_old)
        alpha_ref[1 - parity] = jnp.exp2(m_old - m_new)
        m_ref[...] = m_new

    def weighted_values(c, parity):
        for cs in col_tiles:
            acc_ref[:, cs] = (alpha_ref[parity, :, cs] * acc_ref[:, cs]
                              + jnp.dot(vT_ref[0, c], bufs[parity][:, cs], preferred_element_type=f32))

    first_masked = seq // tk
    n_pairs = (first_masked - 1) // 2
    acc_ref[...] = jnp.zeros(acc_ref.shape, f32)
    alpha_ref[...] = jnp.ones(alpha_ref.shape, f32)
    jump_ref[...] = jnp.zeros(jump_ref.shape, f32)
    m_ref[...] = jnp.max(jnp.dot(k_ref[0, 0:BF16_ROWS, :], qq_ref[...], preferred_element_type=f32),
                         axis=0, keepdims=True)
    probs(0, 0, first_masked == 0)

    def pair(i, carry):
        for half in range(2):
            weighted_values(2 * i + half, half)
            probs(2 * i + half + 1, 1 - half, False)
        return carry

    lax.fori_loop(0, n_pairs, pair, 0)
    for c in range(2 * n_pairs, n_chunks):
        weighted_values(c, c % 2)
        if c + 1 < n_chunks:
            probs(c + 1, (c + 1) % 2, c + 1 >= first_masked)

    @pl.when(jnp.max(jump_ref[...]) > EXP2_SAFE_JUMP)
    def _():
        acc_ref[...] = jnp.zeros(acc_ref.shape, f32)
        m_ref[...] = jnp.full(m_ref.shape, MASK_VALUE, f32)

        def exact_chunk(c, carry):
            ks = pl.multiple_of(c * tk, tk)
            krow = ks + lax.broadcasted_iota(i32, (tk, 1), 0)
            valid = (krow < seq) | (krow >= lpp - N_META)
            for cs in col_tiles:
                s = jnp.where(valid, masked_scores(c, cs, False), MASK_VALUE)
                m_old = m_ref[:, cs]
                m_new = jnp.maximum(m_old, jnp.max(s, axis=0, keepdims=True))
                p = jnp.exp2(s - m_new).astype(bf16)
                acc_ref[:, cs] = (jnp.exp2(m_old - m_new) * acc_ref[:, cs]
                                  + jnp.dot(vT_ref[0, c], p, preferred_element_type=f32))
                m_ref[:, cs] = m_new
            return carry

        lax.fori_loop(0, n_chunks, exact_chunk, 0)

    lam = (jnp.exp(jnp.sum(lq1_ref[...] * lk1_ref[...], axis=-1, keepdims=True))
           - jnp.exp(jnp.sum(lq2_ref[...] * lk2_ref[...], axis=-1, keepdims=True)) + LAM_INIT)
    acc = acc_ref[...]
    o = acc[:2 * hd, :] / acc[2 * hd:2 * hd + 1, :]
    d = (o[:, :tq] - lam * o[:, tq:]).T
    o_ref[0] = (_rms(d, sg_ref[...]) * (1.0 - LAM_INIT)).astype(bf16)


def _outproj_kernel(attn_ref, u_ref, up_ref, un_ref, cb_ref, cw_ref, x_ref, tail_ref, wo_ref, g2_ref, wrT_ref,
                    h1_ref, hn2t_ref, affT_ref, *, n_full):
    tm = u_ref.shape[1]
    d_model = x_ref.shape[2]
    u = u_ref[0].astype(f32)
    prev_row = up_ref[0][BF16_ROWS - 1:BF16_ROWS, :].astype(f32)
    next_row = un_ref[0][0:1, :].astype(f32)
    rid = lax.broadcasted_iota(i32, (tm, 1), 0)
    um1 = jnp.where(rid == 0, prev_row, pltpu.roll(u, 1, 0))
    up1 = jnp.where(rid == tm - 1, next_row, pltpu.roll(u, tm - 1, 0))
    w = cw_ref[...]
    conv = cb_ref[0].astype(f32) * (w[0:1] * um1 + w[1:2] * u + w[2:3] * up1)
    mixed = jnp.concatenate([attn_ref[0], conv.astype(bf16)], axis=1)
    h1 = _residual_tile(x_ref, tail_ref, n_full) + jnp.dot(mixed, wo_ref[...], preferred_element_type=f32)
    h1_ref[0] = h1
    hn2 = _rms(h1, g2_ref[...])
    for c in range(d_model // LANE):
        hn2t_ref[0, pl.ds(c, tm, stride=SUBLANE), :] = hn2[:, c * LANE:(c + 1) * LANE]
    logits = lax.dot_general(wrT_ref[...], hn2.astype(bf16), (((1,), (1,)), ((), ())),
                             preferred_element_type=f32)
    p = jnp.exp(logits - jnp.max(logits, axis=0, keepdims=True))
    affT_ref[0] = p / jnp.sum(p, axis=0, keepdims=True)


def _route_kernel(aff_ref, idxT_ref, *, seq, lpp, cap):
    n_exp, ntl = aff_ref.shape[1], aff_ref.shape[2]
    cpad = idxT_ref.shape[1]
    tok = (lax.broadcasted_iota(i32, (ntl, LANE), 0) * LANE + lax.broadcasted_iota(i32, (ntl, LANE), 1))
    valid = (tok < seq) | ((tok >= lpp - N_META) & (tok < lpp))
    a = jnp.where(valid[None], aff_ref[0], -1.0)

    def count(mask):
        return jnp.sum(jnp.sum(mask.astype(f32), axis=1, keepdims=True), axis=2, keepdims=True)

    def bisect(i, ans):
        cand = ans | jnp.left_shift(jnp.int32(1), 30 - i)
        keep = count(a >= lax.bitcast_convert_type(cand, f32)) >= cap
        return jnp.where(keep, cand, ans)

    tau = lax.bitcast_convert_type(lax.fori_loop(0, 31, bisect, jnp.zeros((n_exp, 1, 1), i32)), f32)
    gt = (a > tau).astype(f32)
    eq = (a == tau).astype(f32)
    need = cap - count(a > tau)

    sq0 = lax.broadcasted_iota(i32, (LANE, LANE), 0)
    sq1 = lax.broadcasted_iota(i32, (LANE, LANE), 1)
    tri = (sq0 <= sq1).astype(bf16)
    pick_last = (sq0 == LANE - 1).astype(bf16)
    before = (lax.broadcasted_iota(i32, (ntl, ntl), 1) < lax.broadcasted_iota(i32, (ntl, ntl), 0)).astype(bf16)

    def prefix_count(m):
        local = jnp.dot(m.astype(bf16), tri, preferred_element_type=f32)
        tot = jnp.dot(local.astype(bf16), pick_last, preferred_element_type=f32)
        base = jnp.dot(before, tot.astype(bf16), preferred_element_type=f32)
        return local + base, base, tot

    rr = lax.broadcasted_iota(i32, (cpad, LANE), 0).astype(f32)
    tile_tok0 = (lax.broadcasted_iota(i32, (cpad, LANE), 1) * LANE).astype(f32)
    diag = lax.broadcasted_iota(i32, (ntl, LANE), 0) == lax.broadcasted_iota(i32, (ntl, LANE), 1)
    lane_e = lax.broadcasted_iota(i32, (cpad, n_exp), 1)
    zpad = jnp.zeros((LANE - ntl, LANE), f32)
    out = jnp.zeros((cpad, n_exp), f32)
    for e in range(n_exp):
        eq_cum, _, _ = prefix_count(eq[e])
        sel = jnp.maximum(gt[e], eq[e] * (eq_cum <= need[e]).astype(f32))
        cum, base, tot = prefix_count(sel)
        t_lo = jnp.sum(jnp.where(diag, base, 0.0), axis=0, keepdims=True)
        t_hi = jnp.sum(jnp.where(diag, base + tot, 0.0), axis=0, keepdims=True)
        in_tile = (t_lo <= rr) & (t_hi > rr)
        hi = jnp.floor(cum * (1.0 / 32.0))
        lo = cum - 32.0 * hi
        rhs = jnp.concatenate([jnp.concatenate([hi, zpad], axis=0), jnp.concatenate([lo, zpad], axis=0)], axis=1)
        g = jnp.dot(in_tile.astype(f32).astype(bf16), rhs.astype(bf16), preferred_element_type=f32)
        tile_cum = 32.0 * g[:, :LANE] + g[:, LANE:]
        pos = jnp.sum((tile_cum <= rr).astype(f32) + jnp.where(in_tile, tile_tok0, 0.0), axis=1, keepdims=True)
        out = jnp.where(lane_e == e, pos, out)
    rank = lax.broadcasted_iota(i32, (cpad, n_exp), 0)
    idxT_ref[0] = jnp.where(rank < cap, out, 0.0).astype(i32)


def _gather_kernel(idx_ref, src_ref, out_ref, tok_ref):
    cpad, d_model = out_ref.shape[2], out_ref.shape[3]

    def body(j, carry):
        for i in range(GATHER_UNROLL):
            r = j * GATHER_UNROLL + i
            t = idx_ref[0, 0, r]
            tok_ref[pl.ds(pl.multiple_of(r * SUBLANE, SUBLANE), SUBLANE), :] = (
                src_ref[0, pl.ds(pl.multiple_of(t * SUBLANE, SUBLANE), SUBLANE), :])
        return carry

    lax.fori_loop(0, cpad // GATHER_UNROLL, body, 0)
    rows = cpad // FFN_ROW_TILES
    for rt in range(FFN_ROW_TILES):
        out_ref[0, 0, rt * rows:(rt + 1) * rows, :] = jnp.concatenate(
            [tok_ref[pl.ds(rt * rows * SUBLANE + c, rows, stride=SUBLANE), :] for c in range(d_model // LANE)],
            axis=1).astype(bf16)


def _ffn_kernel(xg_ref, wg_ref, wu_ref, wd_ref, wr_ref, y_ref):
    e = pl.program_id(0)
    row_tile = xg_ref.shape[2] // FFN_ROW_TILES
    n_exp = wr_ref.shape[1]
    for rt in range(FFN_ROW_TILES):
        rows = slice(rt * row_tile, (rt + 1) * row_tile)
        x = xg_ref[0, 0, rows, :]
        a = jnp.dot(x, wg_ref[0], preferred_element_type=f32)
        u = jnp.dot(x, wu_ref[0], preferred_element_type=f32)
        act = (a * jax.nn.sigmoid(a) * u).astype(bf16)
        y = jnp.dot(act, wd_ref[0], preferred_element_type=f32)
        lg = jnp.dot(x, wr_ref[...], preferred_element_type=f32)
        p = jnp.exp(lg - jnp.max(lg, axis=-1, keepdims=True))
        lane = lax.broadcasted_iota(i32, (row_tile, n_exp), 1)
        gate = jnp.sum(jnp.where(lane == e, p, 0.0), axis=-1, keepdims=True) / jnp.sum(p, axis=-1, keepdims=True)
        y_ref[0, 0, rows, :] = (y * gate).astype(bf16)


def _combine_kernel(idx_ref, y_ref, h1_ref, gf_ref, out_ref, acc_ref, yt_ref, *, cap, n_exp):
    s = pl.program_id(1)
    tt = out_ref.shape[1]
    d_model = out_ref.shape[2]
    cpad = y_ref.shape[2]
    zrows = 64 * SUBLANE

    @pl.when(s == 0)
    def _():
        def zero(i, carry):
            acc_ref[pl.ds(pl.multiple_of(i * zrows, zrows), zrows), :] = jnp.zeros((zrows, LANE), f32)
            return carry
        lax.fori_loop(0, acc_ref.shape[0] // zrows, zero, 0)

    def add_rows(ranks):
        offs = [pl.multiple_of(idx_ref[0, 0, r] * SUBLANE, SUBLANE) for r in ranks]
        vals = [acc_ref[pl.ds(o, SUBLANE), :]
                + yt_ref[pl.ds(r * SUBLANE if isinstance(r, int) else pl.multiple_of(r * SUBLANE, SUBLANE),
                               SUBLANE), :]
                for o, r in zip(offs, ranks)]
        for o, v in zip(offs, vals):
            acc_ref[pl.ds(o, SUBLANE), :] = v

    @pl.when(s < n_exp)
    def _():
        for c in range(d_model // LANE):
            yt_ref[pl.ds(c, cpad, stride=SUBLANE), :] = y_ref[0, 0, :, c * LANE:(c + 1) * LANE].astype(f32)

        def body(j, carry):
            add_rows([j * GATHER_UNROLL + i for i in range(GATHER_UNROLL)])
            return carry
        full = cap // GATHER_UNROLL
        lax.fori_loop(0, full, body, 0)
        if cap % GATHER_UNROLL:
            add_rows(list(range(full * GATHER_UNROLL, cap)))

    @pl.when(s >= n_exp)
    def _():
        base = pl.multiple_of((s - n_exp) * (tt * SUBLANE), tt * SUBLANE)
        moe = jnp.concatenate([acc_ref[pl.ds(base + c, tt, stride=SUBLANE), :] for c in range(d_model // LANE)],
                              axis=1)
        out_ref[0] = _rms(h1_ref[0] + moe, gf_ref[...])


def kernel(x, meta_tokens, mix_norm_g, w_in, conv_w, lambda_q1, lambda_k1, lambda_q2, lambda_k2, attn_subln_g,
           w_out, ffn_norm_g, w_router, w_gate, w_up, w_down, final_norm_g):
    batch, seq, d_model = x.shape
    assert w_in.shape[0] == 1, "single-layer block"
    hd = lambda_q1.shape[-1]
    cw = conv_w.shape[-1]
    aw = d_model - cw
    n_heads = aw // (2 * hd)
    n_exp = w_router.shape[-1]
    d_ff = w_gate.shape[-1]
    assert 2 * hd == LANE and aw % LANE == 0 and cw % LANE == 0 and d_model == SUBLANE * LANE
    assert conv_w.shape[1] == 3
    length = N_META + seq
    cap = EC_CAPACITY_FACTOR * length // n_exp
    cpad = _round_up(cap, FFN_ROW_TILES * BF16_ROWS)
    tm = ROW_TILE
    lpp = _round_up(length + 1, tm)
    nt = lpp // tm
    tt = next(t for t in (512, 256, 128) if seq % t == 0)
    vrows = LANE + ONES_ROWS

    n_full = seq // tm
    assert n_full >= 1 and seq >= BF16_ROWS
    tail = jnp.concatenate([x[:, n_full * tm:], jnp.zeros((batch, lpp - length, d_model), x.dtype),
                            jnp.broadcast_to(meta_tokens.astype(x.dtype)[None], (batch, N_META, d_model))], axis=1)
    x_spec = pl.BlockSpec((1, tm, d_model), lambda b, i: (b, jnp.minimum(i, n_full - 1), 0))
    tail_spec = pl.BlockSpec((1, tm, d_model), lambda b, i: (b, jnp.maximum(i - n_full, 0), 0))
    pos =((jnp.arange(lpp) + N_META) % lpp).astype(f32)
    inv_freq = ROPE_THETA ** (-jnp.arange(0, hd, 2, dtype=f32) / hd)
    ang = pos[:, None] * inv_freq[None, :]
    cos_t = jnp.tile(jnp.cos(ang), (1, LANE // (hd // 2)))
    sin_t = jnp.tile(jnp.concatenate([-jnp.sin(ang), jnp.sin(ang)], axis=-1), (1, LANE // hd))

    row2 = lambda v: v.reshape(1, -1).astype(f32)

    qT, k, vT, u, cb = pl.pallas_call(
        functools.partial(_inproj_kernel, aw=aw, cw=cw, hd=hd, n_full=n_full),
        grid=(batch, nt),
        in_specs=[x_spec, tail_spec,
                  pl.BlockSpec((1, d_model), lambda b, i: (0, 0)),
                  pl.BlockSpec((d_model, 3 * aw + 3 * cw), lambda b, i: (0, 0)),
                  pl.BlockSpec((tm, LANE), lambda b, i: (i, 0)),
                  pl.BlockSpec((tm, LANE), lambda b, i: (i, 0))],
        out_specs=[pl.BlockSpec((1, 1, aw, tm), lambda b, i: (b, i, 0, 0)),
                   pl.BlockSpec((1, tm, aw), lambda b, i: (b, i, 0)),
                   pl.BlockSpec((1, 1, n_heads * vrows, tm), lambda b, i: (b, i, 0, 0)),
                   pl.BlockSpec((1, tm, cw), lambda b, i: (b, i, 0)),
                   pl.BlockSpec((1, tm, cw), lambda b, i: (b, i, 0))],
        out_shape=[jax.ShapeDtypeStruct((batch, nt, aw, tm), bf16),
                   jax.ShapeDtypeStruct((batch, lpp, aw), bf16),
                   jax.ShapeDtypeStruct((batch, nt, n_heads * vrows, tm), bf16),
                   jax.ShapeDtypeStruct((batch, lpp, cw), bf16),
                   jax.ShapeDtypeStruct((batch, lpp, cw), bf16)],
        compiler_params=_params("parallel", "parallel"),
        name="inproj",
    )(x, tail, row2(mix_norm_g[0]), w_in[0].astype(bf16), cos_t, sin_t)

    lam_spec = pl.BlockSpec((1, hd), lambda b, hh, i: (0, 0))
    attn = pl.pallas_call(
        functools.partial(_attn_kernel, seq=seq, lpp=lpp, hd=hd),
        grid=(batch, n_heads, nt),
        in_specs=[pl.BlockSpec((1, 1, LANE, tm), lambda b, hh, i: (b, i, hh, 0)),
                  pl.BlockSpec((1, lpp, LANE), lambda b, hh, i: (b, 0, hh)),
                  pl.BlockSpec((1, nt, vrows, tm), lambda b, hh, i: (b, 0, hh, 0)),
                  lam_spec, lam_spec, lam_spec, lam_spec,
                  pl.BlockSpec((1, LANE), lambda b, hh, i: (0, 0))],
        out_specs=pl.BlockSpec((1, tm, LANE), lambda b, hh, i: (b, i, hh)),
        out_shape=jax.ShapeDtypeStruct((batch, lpp, aw), bf16),
        scratch_shapes=[pltpu.VMEM((LANE, 2 * tm), bf16), pltpu.VMEM((tm, 2 * tm), bf16),
                        pltpu.VMEM((tm, 2 * tm), bf16), pltpu.VMEM((vrows, 2 * tm), f32),
                        pltpu.VMEM((1, 2 * tm), f32), pltpu.VMEM((1, 2 * tm), f32),
                        pltpu.VMEM((2, 1, 2 * tm), f32), pltpu.VMEM((1, 2 * tm), f32)],
        compiler_params=_params("parallel", "parallel", "parallel"),
        name="diff_attn",
    )(qT, k, vT, row2(lambda_q1[0]), row2(lambda_k1[0]), row2(lambda_q2[0]), row2(lambda_k2[0]),
      row2(attn_subln_g[0]))

    halo = tm // BF16_ROWS
    n_halo = lpp // BF16_ROWS
    h1, hn2t, affT = pl.pallas_call(
        functools.partial(_outproj_kernel, n_full=n_full),
        grid=(batch, nt),
        in_specs=[pl.BlockSpec((1, tm, aw), lambda b, i: (b, i, 0)),
                  pl.BlockSpec((1, tm, cw), lambda b, i: (b, i, 0)),
                  pl.BlockSpec((1, BF16_ROWS, cw), lambda b, i: (b, (i * halo + n_halo - 1) % n_halo, 0)),
                  pl.BlockSpec((1, BF16_ROWS, cw), lambda b, i: (b, ((i + 1) * halo) % n_halo, 0)),
                  pl.BlockSpec((1, tm, cw), lambda b, i: (b, i, 0)),
                  pl.BlockSpec((3, cw), lambda b, i: (0, 0)),
                  x_spec, tail_spec,
                  pl.BlockSpec((aw + cw, d_model), lambda b, i: (0, 0)),
                  pl.BlockSpec((1, d_model), lambda b, i: (0, 0)),
                  pl.BlockSpec((n_exp, d_model), lambda b, i: (0, 0))],
        out_specs=[pl.BlockSpec((1, tm, d_model), lambda b, i: (b, i, 0)),
                   pl.BlockSpec((1, tm * SUBLANE, LANE), lambda b, i: (b, i, 0)),
                   pl.BlockSpec((1, n_exp, tm), lambda b, i: (b, 0, i))],
        out_shape=[jax.ShapeDtypeStruct((batch, lpp, d_model), f32),
                   jax.ShapeDtypeStruct((batch, lpp * SUBLANE, LANE), f32),
                   jax.ShapeDtypeStruct((batch, n_exp, lpp), f32)],
        compiler_params=_params("parallel", "parallel"),
        name="outproj_router",
    )(attn, u, u, u, cb, conv_w[0].astype(f32), x, tail, w_out[0].astype(bf16), row2(ffn_norm_g[0]),
      w_router[0].T.astype(bf16))

    ntl = _round_up(lpp // LANE, SUBLANE)
    assert ntl <= LANE
    aff_tiles = jnp.pad(affT, ((0, 0), (0, 0), (0, ntl * LANE - lpp))).reshape(batch, n_exp, ntl, LANE)
    idxT = pl.pallas_call(
        functools.partial(_route_kernel, seq=seq, lpp=lpp, cap=cap),
        grid=(batch,),
        in_specs=[pl.BlockSpec((1, n_exp, ntl, LANE), lambda b: (b, 0, 0, 0))],
        out_specs=pl.BlockSpec((1, cpad, n_exp), lambda b: (b, 0, 0)),
        out_shape=jax.ShapeDtypeStruct((batch, cpad, n_exp), i32),
        compiler_params=_params("parallel"),
        name="expert_choice",
    )(aff_tiles)
    idx = jnp.swapaxes(idxT, 1, 2).reshape(batch * n_exp, 1, cpad)

    idx_spec = lambda im: pl.BlockSpec((1, 1, cpad), im, memory_space=pltpu.SMEM)
    xg = pl.pallas_call(
        _gather_kernel,
        grid=(batch, n_exp),
        in_specs=[idx_spec(lambda b, e: (b * n_exp + e, 0, 0)),
                  pl.BlockSpec((1, lpp * SUBLANE, LANE), lambda b, e: (b, 0, 0), pipeline_mode=pl.Buffered(1))],
        out_specs=pl.BlockSpec((1, 1, cpad, d_model), lambda b, e: (b, e, 0, 0)),
        out_shape=jax.ShapeDtypeStruct((batch, n_exp, cpad, d_model), bf16),
        scratch_shapes=[pltpu.VMEM((cpad * SUBLANE, LANE), f32)],
        compiler_params=_params("parallel", "arbitrary"),
        name="token_gather",
    )(idx, hn2t)

    y = pl.pallas_call(
        _ffn_kernel,
        grid=(n_exp, batch),
        in_specs=[pl.BlockSpec((1, 1, cpad, d_model), lambda e, b: (b, e, 0, 0)),
                  pl.BlockSpec((1, d_model, d_ff), lambda e, b: (e, 0, 0)),
                  pl.BlockSpec((1, d_model, d_ff), lambda e, b: (e, 0, 0)),
                  pl.BlockSpec((1, d_ff, d_model), lambda e, b: (e, 0, 0)),
                  pl.BlockSpec((d_model, n_exp), lambda e, b: (0, 0))],
        out_specs=pl.BlockSpec((1, 1, cpad, d_model), lambda e, b: (b, e, 0, 0)),
        out_shape=jax.ShapeDtypeStruct((batch, n_exp, cpad, d_model), bf16),
        compiler_params=_params("parallel", "parallel"),
        name="expert_ffn",
    )(xg, w_gate[0].astype(bf16), w_up[0].astype(bf16), w_down[0].astype(bf16), w_router[0].astype(bf16))

    n_out = seq // tt
    last = n_exp - 1
    out = pl.pallas_call(
        functools.partial(_combine_kernel, cap=cap, n_exp=n_exp),
        grid=(batch, n_exp + n_out),
        in_specs=[idx_spec(lambda b, s: (b * n_exp + jnp.minimum(s, last), 0, 0)),
                  pl.BlockSpec((1, 1, cpad, d_model), lambda b, s: (b, jnp.minimum(s, last), 0, 0)),
                  pl.BlockSpec((1, tt, d_model), lambda b, s: (b, jnp.maximum(s - n_exp, 0), 0)),
                  pl.BlockSpec((1, d_model), lambda b, s: (0, 0))],
        out_specs=pl.BlockSpec((1, tt, d_model), lambda b, s: (b, jnp.maximum(s - n_exp, 0), 0)),
        out_shape=jax.ShapeDtypeStruct((batch, seq, d_model), x.dtype),
        scratch_shapes=[pltpu.VMEM((lpp * SUBLANE, LANE), f32), pltpu.VMEM((cpad * SUBLANE, LANE), f32)],
        compiler_params=_params("parallel", "arbitrary"),
        name="combine_norm",
    )(idx, y, h1, row2(final_norm_g))
    return out
```

```python
import functools

import jax
import jax.numpy as jnp
from jax import lax
from jax.experimental import pallas as pl
from jax.experimental.pallas import tpu as pltpu

N_META = 16
EC_CAPACITY_FACTOR = 2
ROPE_THETA = 10000.0
NORM_EPS = 1e-6
LAM_INIT = 0.8 - 0.6 * 1.0

LANE = 128
SUBLANE = 8
BF16_ROWS = 16
VMEM_LIMIT = 56 * 1024 * 1024

ROW_TILE = 768
MASK_VALUE = -1e30
GATHER_UNROLL = 8
ONES_ROWS = BF16_ROWS
FFN_ROW_TILES = 3
ATTN_COL_TILE = 256
ATTN_KEY_CHUNK = ROW_TILE
LOG2_E = 1.4426950408889634
EXP2_SAFE_JUMP = 100.0

f32 = jnp.float32
bf16 = jnp.bfloat16
i32 = jnp.int32


def _round_up(a, m):
    return -(-a // m) * m


def _params(*sem):
    return pltpu.CompilerParams(dimension_semantics=sem, vmem_limit_bytes=VMEM_LIMIT)


def _rms(x, g):
    return x * lax.rsqrt(jnp.mean(x * x, axis=-1, keepdims=True) + NORM_EPS) * g


def _residual_tile(x_ref, tail_ref, n_full):
    return jnp.where(pl.program_id(1) < n_full, x_ref[0], tail_ref[0])


def _inproj_kernel(x_ref, tail_ref, g_ref, w_ref, cos_ref, sin_ref, qT_ref, k_ref, vT_ref, u_ref, cb_ref,
                   *, aw, cw, hd, n_full):
    tm = x_ref.shape[1]
    hn = _rms(_residual_tile(x_ref, tail_ref, n_full), g_ref[...]).astype(bf16)
    proj = jnp.dot(hn, w_ref[...], preferred_element_type=f32)
    cos = cos_ref[...]
    sin = sin_ref[...]
    lane = lax.broadcasted_iota(i32, (tm, LANE), 1)
    first_half = (lane % hd) < (hd // 2)

    def rope(x):
        rot = jnp.where(first_half, pltpu.roll(x, LANE - hd // 2, 1), pltpu.roll(x, hd // 2, 1))
        return x * cos + rot * sin

    scale = hd ** -0.5 * LOG2_E
    vrows = LANE + ONES_ROWS
    for hb in range(aw // LANE):
        sl = slice(hb * LANE, (hb + 1) * LANE)
        q = rope(proj[:, sl]) * scale
        qT_ref[0, 0, sl, :] = q.T.astype(bf16)
        k = rope(proj[:, aw + hb * LANE:aw + (hb + 1) * LANE])
        k_ref[0, :, sl] = k.astype(bf16)
        v = proj[:, 2 * aw + hb * LANE:2 * aw + (hb + 1) * LANE]
        vT_ref[0, 0, hb * vrows:hb * vrows + LANE, :] = v.T.astype(bf16)
        vT_ref[0, 0, hb * vrows + LANE:(hb + 1) * vrows, :] = jnp.ones((ONES_ROWS, tm), bf16)
    cx = proj[:, 3 * aw:3 * aw + cw]
    cb = proj[:, 3 * aw + cw:3 * aw + 2 * cw]
    cc = proj[:, 3 * aw + 2 * cw:]
    u_ref[0] = (cc * cx).astype(bf16)
    cb_ref[0] = cb.astype(bf16)


def _attn_kernel(qT_ref, k_ref, vT_ref, lq1_ref, lk1_ref, lq2_ref, lk2_ref, sg_ref, wg_ref, wu_ref, wd_ref,
                 o_ref, wg16_ref, wu16_ref, wd16_ref,
                 qq_ref, pa_ref, pb_ref, acc_ref, m_ref, cm_ref, alpha_ref, jump_ref, *, seq, lpp, hd):
    wg16_ref[...] = wg_ref[...].astype(bf16)
    wu16_ref[...] = wu_ref[...].astype(bf16)
    wd16_ref[...] = wd_ref[...].astype(bf16)
    tq = qT_ref.shape[3]
    n_tiles, tm = vT_ref.shape[1], vT_ref.shape[3]
    tk = pa_ref.shape[0]
    sub = tm // tk
    assert sub in (1, 2) and sub * tk == tm
    n_chunks = n_tiles * sub
    col_tiles = [slice(j, j + ATTN_COL_TILE) for j in range(0, 2 * tq, ATTN_COL_TILE)]
    qt = qT_ref[0, 0]
    row = lax.broadcasted_iota(i32, qt.shape, 0)
    zero = jnp.zeros_like(qt)
    qq_ref[:, :tq] = jnp.where(row < hd, qt, zero)
    qq_ref[:, tq:] = jnp.where(row >= hd, qt, zero)

    def key_start(c):
        return c * tk if isinstance(c, int) else pl.multiple_of(c * tk, tk)

    def masked_scores(c, cs, masked):
        s = jnp.dot(k_ref[0, pl.ds(key_start(c), tk), :], qq_ref[:, cs], preferred_element_type=f32)
        if masked:
            krow = key_start(c) + lax.broadcasted_iota(i32, (tk, 1), 0)
            s = jnp.where((krow < seq) | (krow >= lpp - N_META), s, MASK_VALUE)
        return s

    bufs = (pa_ref, pb_ref)

    def probs(c, parity, masked):
        for cs in col_tiles:
            s = masked_scores(c, cs, masked)
            bufs[parity][:, cs] = jnp.exp2(s - m_ref[:, cs]).astype(bf16)
            cm_ref[:, cs] = jnp.max(s, axis=0, keepdims=True)
        m_old = m_ref[...]
        m_new = jnp.maximum(m_old, cm_ref[...])
        jump_ref[...] = jnp.maximum(jump_ref[...], cm_ref[...] - m_old)
        alpha_ref[1 - parity] = jnp.exp2(m_old - m_new)
        m_ref[...] = m_new

    def weighted_values(tile, part, parity):
        vt = vT_ref[0, tile, :, part * tk:(part + 1) * tk]
        for cs in col_tiles:
            acc_ref[:, cs] = (alpha_ref[parity, :, cs] * acc_ref[:, cs]
                              + jnp.dot(vt, bufs[parity][:, cs], preferred_element_type=f32))

    first_masked = seq // tk
    n_pairs = (first_masked - 1) // 2
    acc_ref[...] = jnp.zeros(acc_ref.shape, f32)
    alpha_ref[...] = jnp.ones(alpha_ref.shape, f32)
    jump_ref[...] = jnp.zeros(jump_ref.shape, f32)
    m_ref[...] = jnp.max(jnp.dot(k_ref[0, 0:BF16_ROWS, :], qq_ref[...], preferred_element_type=f32),
                         axis=0, keepdims=True)
    probs(0, 0, first_masked == 0)

    def pair(i, carry):
        for half in range(2):
            c = 2 * i + half
            weighted_values(*((c, 0) if sub == 1 else (i, half)), half)
            probs(c + 1, 1 - half, False)
        return carry

    lax.fori_loop(0, n_pairs, pair, 0)
    for c in range(2 * n_pairs, n_chunks):
        weighted_values(c // sub, c % sub, c % 2)
        if c + 1 < n_chunks:
            probs(c + 1, (c + 1) % 2, c + 1 >= first_masked)

    @pl.when(jnp.max(jump_ref[...]) > EXP2_SAFE_JUMP)
    def _():
        acc_ref[...] = jnp.zeros(acc_ref.shape, f32)
        m_ref[...] = jnp.full(m_ref.shape, MASK_VALUE, f32)

        def exact_chunk(c, carry):
            ks = pl.multiple_of(c * tm, tm)
            krow = ks + lax.broadcasted_iota(i32, (tm, 1), 0)
            valid = (krow < seq) | (krow >= lpp - N_META)
            for cs in col_tiles:
                s = jnp.dot(k_ref[0, pl.ds(ks, tm), :], qq_ref[:, cs], preferred_element_type=f32)
                s = jnp.where(valid, s, MASK_VALUE)
                m_old = m_ref[:, cs]
                m_new = jnp.maximum(m_old, jnp.max(s, axis=0, keepdims=True))
                p = jnp.exp2(s - m_new).astype(bf16)
                acc_ref[:, cs] = (jnp.exp2(m_old - m_new) * acc_ref[:, cs]
                                  + jnp.dot(vT_ref[0, c], p, preferred_element_type=f32))
                m_ref[:, cs] = m_new
            return carry

        lax.fori_loop(0, n_tiles, exact_chunk, 0)

    lam = (jnp.exp(jnp.sum(lq1_ref[...] * lk1_ref[...], axis=-1, keepdims=True))
           - jnp.exp(jnp.sum(lq2_ref[...] * lk2_ref[...], axis=-1, keepdims=True)) + LAM_INIT)
    acc = acc_ref[...]
    o = acc[:2 * hd, :] / acc[2 * hd:2 * hd + 1, :]
    d = (o[:, :tq] - lam * o[:, tq:]).T
    o_ref[0] = (_rms(d, sg_ref[...]) * (1.0 - LAM_INIT)).astype(bf16)


def _outproj_kernel(attn_ref, u_ref, up_ref, un_ref, cb_ref, cw_ref, x_ref, tail_ref, wo_ref, g2_ref, wrT_ref,
                    h1_ref, hn2t_ref, affT_ref, *, n_full):
    tm = u_ref.shape[1]
    d_model = x_ref.shape[2]
    u = u_ref[0].astype(f32)
    prev_row = up_ref[0][BF16_ROWS - 1:BF16_ROWS, :].astype(f32)
    next_row = un_ref[0][0:1, :].astype(f32)
    rid = lax.broadcasted_iota(i32, (tm, 1), 0)
    um1 = jnp.where(rid == 0, prev_row, pltpu.roll(u, 1, 0))
    up1 = jnp.where(rid == tm - 1, next_row, pltpu.roll(u, tm - 1, 0))
    w = cw_ref[...]
    conv = cb_ref[0].astype(f32) * (w[0:1] * um1 + w[1:2] * u + w[2:3] * up1)
    mixed = jnp.concatenate([attn_ref[0], conv.astype(bf16)], axis=1)
    h1 = _residual_tile(x_ref, tail_ref, n_full) + jnp.dot(mixed, wo_ref[...], preferred_element_type=f32)
    h1_ref[0] = h1
    hn2 = _rms(h1, g2_ref[...])
    for c in range(d_model // LANE):
        hn2t_ref[0, pl.ds(c, tm, stride=SUBLANE), :] = hn2[:, c * LANE:(c + 1) * LANE]
    logits = lax.dot_general(wrT_ref[...], hn2.astype(bf16), (((1,), (1,)), ((), ())),
                             preferred_element_type=f32)
    p = jnp.exp(logits - jnp.max(logits, axis=0, keepdims=True))
    affT_ref[0] = p / jnp.sum(p, axis=0, keepdims=True)


def _route_kernel(aff_ref, idxT_ref, *, seq, lpp, cap):
    n_exp, ntl = aff_ref.shape[1], aff_ref.shape[2]
    cpad = idxT_ref.shape[1]
    tok = (lax.broadcasted_iota(i32, (ntl, LANE), 0) * LANE + lax.broadcasted_iota(i32, (ntl, LANE), 1))
    valid = (tok < seq) | ((tok >= lpp - N_META) & (tok < lpp))
    a = jnp.where(valid[None], aff_ref[0], -1.0)

    def count(mask):
        return jnp.sum(jnp.sum(mask.astype(f32), axis=1, keepdims=True), axis=2, keepdims=True)

    def bisect(i, ans):
        cand = ans | jnp.left_shift(jnp.int32(1), 30 - i)
        keep = count(a >= lax.bitcast_convert_type(cand, f32)) >= cap
        return jnp.where(keep, cand, ans)

    tau = lax.bitcast_convert_type(lax.fori_loop(0, 31, bisect, jnp.zeros((n_exp, 1, 1), i32)), f32)
    gt = (a > tau).astype(f32)
    eq = (a == tau).astype(f32)
    need = cap - count(a > tau)

    sq0 = lax.broadcasted_iota(i32, (LANE, LANE), 0)
    sq1 = lax.broadcasted_iota(i32, (LANE, LANE), 1)
    tri = (sq0 <= sq1).astype(bf16)
    pick_last = (sq0 == LANE - 1).astype(bf16)
    before = (lax.broadcasted_iota(i32, (ntl, ntl), 1) < lax.broadcasted_iota(i32, (ntl, ntl), 0)).astype(bf16)

    def prefix_count(m):
        local = jnp.dot(m.astype(bf16), tri, preferred_element_type=f32)
        tot = jnp.dot(local.astype(bf16), pick_last, preferred_element_type=f32)
        base = jnp.dot(before, tot.astype(bf16), preferred_element_type=f32)
        return local + base, base, tot

    rr = lax.broadcasted_iota(i32, (cpad, LANE), 0).astype(f32)
    tile_tok0 = (lax.broadcasted_iota(i32, (cpad, LANE), 1) * LANE).astype(f32)
    diag = lax.broadcasted_iota(i32, (ntl, LANE), 0) == lax.broadcasted_iota(i32, (ntl, LANE), 1)
    lane_e = lax.broadcasted_iota(i32, (cpad, n_exp), 1)
    zpad = jnp.zeros((LANE - ntl, LANE), f32)
    out = jnp.zeros((cpad, n_exp), f32)
    for e in range(n_exp):
        eq_cum, _, _ = prefix_count(eq[e])
        sel = jnp.maximum(gt[e], eq[e] * (eq_cum <= need[e]).astype(f32))
        cum, base, tot = prefix_count(sel)
        t_lo = jnp.sum(jnp.where(diag, base, 0.0), axis=0, keepdims=True)
        t_hi = jnp.sum(jnp.where(diag, base + tot, 0.0), axis=0, keepdims=True)
        in_tile = (t_lo <= rr) & (t_hi > rr)
        hi = jnp.floor(cum * (1.0 / 32.0))
        lo = cum - 32.0 * hi
        rhs = jnp.concatenate([jnp.concatenate([hi, zpad], axis=0), jnp.concatenate([lo, zpad], axis=0)], axis=1)
        g = jnp.dot(in_tile.astype(f32).astype(bf16), rhs.astype(bf16), preferred_element_type=f32)
        tile_cum = 32.0 * g[:, :LANE] + g[:, LANE:]
        pos = jnp.sum((tile_cum <= rr).astype(f32) + jnp.where(in_tile, tile_tok0, 0.0), axis=1, keepdims=True)
        out = jnp.where(lane_e == e, pos, out)
    rank = lax.broadcasted_iota(i32, (cpad, n_exp), 0)
    idxT_ref[0] = jnp.where(rank < cap, out, 0.0).astype(i32)


def _gather_kernel(idx_ref, src_ref, out_ref):
    cpad = idx_ref.shape[2]

    def body(j, carry):
        for i in range(GATHER_UNROLL):
            r = j * GATHER_UNROLL + i
            t = idx_ref[0, 0, r]
            out_ref[0, 0, pl.ds(pl.multiple_of(r * SUBLANE, SUBLANE), SUBLANE), :] = (
                src_ref[0, pl.ds(pl.multiple_of(t * SUBLANE, SUBLANE), SUBLANE), :])
        return carry

    lax.fori_loop(0, cpad // GATHER_UNROLL, body, 0)


def _ffn_kernel(xg_ref, wg_ref, wu_ref, wd_ref, wr_ref, y_ref):
    e = pl.program_id(0)
    row_tile = xg_ref.shape[2] // (SUBLANE * FFN_ROW_TILES)
    d_model = wg_ref.shape[1]
    n_exp = wr_ref.shape[1]
    nc = d_model // LANE
    for rt in range(FFN_ROW_TILES):
        base = rt * row_tile * SUBLANE
        x = jnp.concatenate([xg_ref[0, 0, pl.ds(base + c, row_tile, stride=SUBLANE), :] for c in range(nc)],
                            axis=1).astype(bf16)
        a = jnp.dot(x, wg_ref[0], preferred_element_type=f32)
        u = jnp.dot(x, wu_ref[0], preferred_element_type=f32)
        act = (a * jax.nn.sigmoid(a) * u).astype(bf16)
        y = jnp.dot(act, wd_ref[0], preferred_element_type=f32)
        lg = jnp.dot(x, wr_ref[...], preferred_element_type=f32)
        p = jnp.exp(lg - jnp.max(lg, axis=-1, keepdims=True))
        lane = lax.broadcasted_iota(i32, (row_tile, n_exp), 1)
        gate = jnp.sum(jnp.where(lane == e, p, 0.0), axis=-1, keepdims=True) / jnp.sum(p, axis=-1, keepdims=True)
        y = y * gate
        for c in range(nc):
            y_ref[0, 0, pl.ds(base + c, row_tile, stride=SUBLANE), :] = y[:, c * LANE:(c + 1) * LANE]


def _combine_kernel(idx_ref, y_ref, h1_ref, gf_ref, out_ref, acc_ref, *, cap, n_exp):
    s = pl.program_id(1)
    tt = out_ref.shape[1]
    d_model = out_ref.shape[2]
    zrows = 64 * SUBLANE

    @pl.when(s == 0)
    def _():
        def zero(i, carry):
            acc_ref[pl.ds(pl.multiple_of(i * zrows, zrows), zrows), :] = jnp.zeros((zrows, LANE), f32)
            return carry
        lax.fori_loop(0, acc_ref.shape[0] // zrows, zero, 0)

    def add_rows(ranks):
        offs = [pl.multiple_of(idx_ref[0, 0, r] * SUBLANE, SUBLANE) for r in ranks]
        vals = [acc_ref[pl.ds(o, SUBLANE), :]
                + y_ref[0, 0, pl.ds(r * SUBLANE if isinstance(r, int) else pl.multiple_of(r * SUBLANE, SUBLANE),
                                    SUBLANE), :]
                for o, r in zip(offs, ranks)]
        for o, v in zip(offs, vals):
            acc_ref[pl.ds(o, SUBLANE), :] = v

    @pl.when(s < n_exp)
    def _():
        def body(j, carry):
            add_rows([j * GATHER_UNROLL + i for i in range(GATHER_UNROLL)])
            return carry
        full = cap // GATHER_UNROLL
        lax.fori_loop(0, full, body, 0)
        if cap % GATHER_UNROLL:
            add_rows(list(range(full * GATHER_UNROLL, cap)))

    @pl.when(s >= n_exp)
    def _():
        base = pl.multiple_of((s - n_exp) * (tt * SUBLANE), tt * SUBLANE)
        moe = jnp.concatenate([acc_ref[pl.ds(base + c, tt, stride=SUBLANE), :] for c in range(d_model // LANE)],
                              axis=1)
        out_ref[0] = _rms(h1_ref[0] + moe, gf_ref[...])


def kernel(x, meta_tokens, mix_norm_g, w_in, conv_w, lambda_q1, lambda_k1, lambda_q2, lambda_k2, attn_subln_g,
           w_out, ffn_norm_g, w_router, w_gate, w_up, w_down, final_norm_g):
    batch, seq, d_model = x.shape
    assert w_in.shape[0] == 1, "single-layer block"
    hd = lambda_q1.shape[-1]
    cw = conv_w.shape[-1]
    aw = d_model - cw
    n_heads = aw // (2 * hd)
    n_exp = w_router.shape[-1]
    d_ff = w_gate.shape[-1]
    assert 2 * hd == LANE and aw % LANE == 0 and cw % LANE == 0 and d_model == SUBLANE * LANE
    assert conv_w.shape[1] == 3
    length = N_META + seq
    cap = EC_CAPACITY_FACTOR * length // n_exp
    cpad = _round_up(cap, FFN_ROW_TILES * SUBLANE)
    tm = ROW_TILE
    lpp = _round_up(length + 1, tm)
    nt = lpp // tm
    tt = next(t for t in (512, 256, 128) if seq % t == 0)
    vrows = LANE + ONES_ROWS

    n_full = seq // tm
    assert n_full >= 1 and seq >= BF16_ROWS
    tail = jnp.concatenate([x[:, n_full * tm:], jnp.zeros((batch, lpp - length, d_model), x.dtype),
                            jnp.broadcast_to(meta_tokens.astype(x.dtype)[None], (batch, N_META, d_model))], axis=1)
    x_spec = pl.BlockSpec((1, tm, d_model), lambda b, i: (b, jnp.minimum(i, n_full - 1), 0))
    tail_spec = pl.BlockSpec((1, tm, d_model), lambda b, i: (b, jnp.maximum(i - n_full, 0), 0))
    pos =((jnp.arange(lpp) + N_META) % lpp).astype(f32)
    inv_freq = ROPE_THETA ** (-jnp.arange(0, hd, 2, dtype=f32) / hd)
    ang = pos[:, None] * inv_freq[None, :]
    cos_t = jnp.tile(jnp.cos(ang), (1, LANE // (hd // 2)))
    sin_t = jnp.tile(jnp.concatenate([-jnp.sin(ang), jnp.sin(ang)], axis=-1), (1, LANE // hd))

    row2 = lambda v: v.reshape(1, -1).astype(f32)

    qT, k, vT, u, cb = pl.pallas_call(
        functools.partial(_inproj_kernel, aw=aw, cw=cw, hd=hd, n_full=n_full),
        grid=(batch, nt),
        in_specs=[x_spec, tail_spec,
                  pl.BlockSpec((1, d_model), lambda b, i: (0, 0)),
                  pl.BlockSpec((d_model, 3 * aw + 3 * cw), lambda b, i: (0, 0)),
                  pl.BlockSpec((tm, LANE), lambda b, i: (i, 0)),
                  pl.BlockSpec((tm, LANE), lambda b, i: (i, 0))],
        out_specs=[pl.BlockSpec((1, 1, aw, tm), lambda b, i: (b, i, 0, 0)),
                   pl.BlockSpec((1, tm, aw), lambda b, i: (b, i, 0)),
                   pl.BlockSpec((1, 1, n_heads * vrows, tm), lambda b, i: (b, i, 0, 0)),
                   pl.BlockSpec((1, tm, cw), lambda b, i: (b, i, 0)),
                   pl.BlockSpec((1, tm, cw), lambda b, i: (b, i, 0))],
        out_shape=[jax.ShapeDtypeStruct((batch, nt, aw, tm), bf16),
                   jax.ShapeDtypeStruct((batch, lpp, aw), bf16),
                   jax.ShapeDtypeStruct((batch, nt, n_heads * vrows, tm), bf16),
                   jax.ShapeDtypeStruct((batch, lpp, cw), bf16),
                   jax.ShapeDtypeStruct((batch, lpp, cw), bf16)],
        compiler_params=_params("parallel", "parallel"),
        name="inproj",
    )(x, tail, row2(mix_norm_g[0]), w_in[0].astype(bf16), cos_t, sin_t)

    attn_steps = batch * n_heads * nt
    n_slabs = 1 << (attn_steps.bit_length() - 1)
    assert (n_exp * d_model) % (n_slabs * BF16_ROWS) == 0 and (n_exp * d_ff) % (n_slabs * BF16_ROWS) == 0
    slab_map = lambda b, hh, i: (jnp.minimum((b * n_heads + hh) * nt + i, n_slabs - 1), 0)
    up_slab = pl.BlockSpec((n_exp * d_model // n_slabs, d_ff), slab_map)
    down_slab = pl.BlockSpec((n_exp * d_ff // n_slabs, d_model), slab_map)

    lam_spec = pl.BlockSpec((1, hd), lambda b, hh, i: (0, 0))
    attn, wg16, wu16, wd16 = pl.pallas_call(
        functools.partial(_attn_kernel, seq=seq, lpp=lpp, hd=hd),
        grid=(batch, n_heads, nt),
        in_specs=[pl.BlockSpec((1, 1, LANE, tm), lambda b, hh, i: (b, i, hh, 0)),
                  pl.BlockSpec((1, lpp, LANE), lambda b, hh, i: (b, 0, hh)),
                  pl.BlockSpec((1, nt, vrows, tm), lambda b, hh, i: (b, 0, hh, 0)),
                  lam_spec, lam_spec, lam_spec, lam_spec,
                  pl.BlockSpec((1, LANE), lambda b, hh, i: (0, 0)),
                  up_slab, up_slab, down_slab],
        out_specs=[pl.BlockSpec((1, tm, LANE), lambda b, hh, i: (b, i, hh)), up_slab, up_slab, down_slab],
        out_shape=[jax.ShapeDtypeStruct((batch, lpp, aw), bf16),
                   jax.ShapeDtypeStruct((n_exp * d_model, d_ff), bf16),
                   jax.ShapeDtypeStruct((n_exp * d_model, d_ff), bf16),
                   jax.ShapeDtypeStruct((n_exp * d_ff, d_model), bf16)],
        scratch_shapes=[pltpu.VMEM((LANE, 2 * tm), bf16), pltpu.VMEM((ATTN_KEY_CHUNK, 2 * tm), bf16),
                        pltpu.VMEM((ATTN_KEY_CHUNK, 2 * tm), bf16), pltpu.VMEM((vrows, 2 * tm), f32),
                        pltpu.VMEM((1, 2 * tm), f32), pltpu.VMEM((1, 2 * tm), f32),
                        pltpu.VMEM((2, 1, 2 * tm), f32), pltpu.VMEM((1, 2 * tm), f32)],
        compiler_params=_params("parallel", "parallel", "parallel"),
        name="diff_attn",
    )(qT, k, vT, row2(lambda_q1[0]), row2(lambda_k1[0]), row2(lambda_q2[0]), row2(lambda_k2[0]),
      row2(attn_subln_g[0]), w_gate[0].reshape(n_exp * d_model, d_ff), w_up[0].reshape(n_exp * d_model, d_ff),
      w_down[0].reshape(n_exp * d_ff, d_model))

    halo = tm // BF16_ROWS
    n_halo = lpp // BF16_ROWS
    h1, hn2t, affT = pl.pallas_call(
        functools.partial(_outproj_kernel, n_full=n_full),
        grid=(batch, nt),
        in_specs=[pl.BlockSpec((1, tm, aw), lambda b, i: (b, i, 0)),
                  pl.BlockSpec((1, tm, cw), lambda b, i: (b, i, 0)),
                  pl.BlockSpec((1, BF16_ROWS, cw), lambda b, i: (b, (i * halo + n_halo - 1) % n_halo, 0)),
                  pl.BlockSpec((1, BF16_ROWS, cw), lambda b, i: (b, ((i + 1) * halo) % n_halo, 0)),
                  pl.BlockSpec((1, tm, cw), lambda b, i: (b, i, 0)),
                  pl.BlockSpec((3, cw), lambda b, i: (0, 0)),
                  x_spec, tail_spec,
                  pl.BlockSpec((aw + cw, d_model), lambda b, i: (0, 0)),
                  pl.BlockSpec((1, d_model), lambda b, i: (0, 0)),
                  pl.BlockSpec((n_exp, d_model), lambda b, i: (0, 0))],
        out_specs=[pl.BlockSpec((1, tm, d_model), lambda b, i: (b, i, 0)),
                   pl.BlockSpec((1, tm * SUBLANE, LANE), lambda b, i: (b, i, 0)),
                   pl.BlockSpec((1, n_exp, tm), lambda b, i: (b, 0, i))],
        out_shape=[jax.ShapeDtypeStruct((batch, lpp, d_model), f32),
                   jax.ShapeDtypeStruct((batch, lpp * SUBLANE, LANE), f32),
                   jax.ShapeDtypeStruct((batch, n_exp, lpp), f32)],
        compiler_params=_params("parallel", "parallel"),
        name="outproj_router",
    )(attn, u, u, u, cb, conv_w[0].astype(f32), x, tail, w_out[0].astype(bf16), row2(ffn_norm_g[0]),
      w_router[0].T.astype(bf16))

    ntl = _round_up(lpp // LANE, SUBLANE)
    assert ntl <= LANE
    aff_tiles = jnp.pad(affT, ((0, 0), (0, 0), (0, ntl * LANE - lpp))).reshape(batch, n_exp, ntl, LANE)
    idxT = pl.pallas_call(
        functools.partial(_route_kernel, seq=seq, lpp=lpp, cap=cap),
        grid=(batch,),
        in_specs=[pl.BlockSpec((1, n_exp, ntl, LANE), lambda b: (b, 0, 0, 0))],
        out_specs=pl.BlockSpec((1, cpad, n_exp), lambda b: (b, 0, 0)),
        out_shape=jax.ShapeDtypeStruct((batch, cpad, n_exp), i32),
        compiler_params=_params("parallel"),
        name="expert_choice",
    )(aff_tiles)
    idx = jnp.swapaxes(idxT, 1, 2).reshape(batch * n_exp, 1, cpad)

    idx_spec = lambda im: pl.BlockSpec((1, 1, cpad), im, memory_space=pltpu.SMEM)
    xg = pl.pallas_call(
        _gather_kernel,
        grid=(batch, n_exp),
        in_specs=[idx_spec(lambda b, e: (b * n_exp + e, 0, 0)),
                  pl.BlockSpec((1, lpp * SUBLANE, LANE), lambda b, e: (b, 0, 0), pipeline_mode=pl.Buffered(1))],
        out_specs=pl.BlockSpec((1, 1, cpad * SUBLANE, LANE), lambda b, e: (b, e, 0, 0)),
        out_shape=jax.ShapeDtypeStruct((batch, n_exp, cpad * SUBLANE, LANE), f32),
        compiler_params=_params("parallel", "arbitrary"),
        name="token_gather",
    )(idx, hn2t)

    y = pl.pallas_call(
        _ffn_kernel,
        grid=(n_exp, batch),
        in_specs=[pl.BlockSpec((1, 1, cpad * SUBLANE, LANE), lambda e, b: (b, e, 0, 0)),
                  pl.BlockSpec((1, d_model, d_ff), lambda e, b: (e, 0, 0)),
                  pl.BlockSpec((1, d_model, d_ff), lambda e, b: (e, 0, 0)),
                  pl.BlockSpec((1, d_ff, d_model), lambda e, b: (e, 0, 0)),
                  pl.BlockSpec((d_model, n_exp), lambda e, b: (0, 0))],
        out_specs=pl.BlockSpec((1, 1, cpad * SUBLANE, LANE), lambda e, b: (b, e, 0, 0)),
        out_shape=jax.ShapeDtypeStruct((batch, n_exp, cpad * SUBLANE, LANE), f32),
        compiler_params=_params("parallel", "parallel"),
        name="expert_ffn",
    )(xg, wg16.reshape(n_exp, d_model, d_ff), wu16.reshape(n_exp, d_model, d_ff),
      wd16.reshape(n_exp, d_ff, d_model), w_router[0].astype(bf16))

    n_out = seq // tt
    last = n_exp - 1
    out = pl.pallas_call(
        functools.partial(_combine_kernel, cap=cap, n_exp=n_exp),
        grid=(batch, n_exp + n_out),
        in_specs=[idx_spec(lambda b, s: (b * n_exp + jnp.minimum(s, last), 0, 0)),
                  pl.BlockSpec((1, 1, cpad * SUBLANE, LANE), lambda b, s: (b, jnp.minimum(s, last), 0, 0)),
                  pl.BlockSpec((1, tt, d_model), lambda b, s: (b, jnp.maximum(s - n_exp, 0), 0)),
                  pl.BlockSpec((1, d_model), lambda b, s: (0, 0))],
        out_specs=pl.BlockSpec((1, tt, d_model), lambda b, s: (b, jnp.maximum(s - n_exp, 0), 0)),
        out_shape=jax.ShapeDtypeStruct((batch, seq, d_model), x.dtype),
        scratch_shapes=[pltpu.VMEM((lpp * SUBLANE, LANE), f32)],
        compiler_params=_params("parallel", "arbitrary"),
        name="combine_norm",
    )(idx, y, h1, row2(final_norm_g))
    return out
```

```python
import functools

import jax
import jax.numpy as jnp
from jax import lax
from jax.experimental import pallas as pl
from jax.experimental.pallas import tpu as pltpu

N_META = 16
EC_CAPACITY_FACTOR = 2
ROPE_THETA = 10000.0
NORM_EPS = 1e-6
LAM_INIT = 0.8 - 0.6 * 1.0

LANE = 128
SUBLANE = 8
BF16_ROWS = 16
VMEM_LIMIT = 56 * 1024 * 1024

ROW_TILE = 768
MASK_VALUE = -1e30
GATHER_UNROLL = 8
SCATTER_UNROLL = 16
ONES_ROWS = BF16_ROWS
FFN_ROW_TILES = 5
ATTN_COL_TILE = 256
LOG2_E = 1.4426950408889634
EXP2_SAFE_JUMP = 100.0

f32 = jnp.float32
bf16 = jnp.bfloat16
i32 = jnp.int32


def _round_up(a, m):
    return -(-a // m) * m


def _params(*sem):
    return pltpu.CompilerParams(dimension_semantics=sem, vmem_limit_bytes=VMEM_LIMIT)


def _rms(x, g):
    return x * lax.rsqrt(jnp.mean(x * x, axis=-1, keepdims=True) + NORM_EPS) * g


def _residual_tile(x_ref, tail_ref, n_full):
    return jnp.where(pl.program_id(1) < n_full, x_ref[0], tail_ref[0])


def _inproj_kernel(x_ref, tail_ref, g_ref, w_ref, cos_ref, sin_ref, qT_ref, k_ref, vT_ref, u_ref, cb_ref,
                   *, aw, cw, hd, n_full):
    tm = x_ref.shape[1]
    hn = _rms(_residual_tile(x_ref, tail_ref, n_full), g_ref[...]).astype(bf16)
    proj = jnp.dot(hn, w_ref[...], preferred_element_type=f32)
    cos = cos_ref[...]
    sin = sin_ref[...]
    lane = lax.broadcasted_iota(i32, (tm, LANE), 1)
    first_half = (lane % hd) < (hd // 2)

    def rope(x):
        rot = jnp.where(first_half, pltpu.roll(x, LANE - hd // 2, 1), pltpu.roll(x, hd // 2, 1))
        return x * cos + rot * sin

    scale = hd ** -0.5 * LOG2_E
    vrows = LANE + ONES_ROWS
    for hb in range(aw // LANE):
        sl = slice(hb * LANE, (hb + 1) * LANE)
        q = rope(proj[:, sl]) * scale
        qT_ref[0, 0, sl, :] = q.T.astype(bf16)
        k = rope(proj[:, aw + hb * LANE:aw + (hb + 1) * LANE])
        k_ref[0, :, sl] = k.astype(bf16)
        v = proj[:, 2 * aw + hb * LANE:2 * aw + (hb + 1) * LANE]
        vT_ref[0, 0, hb * vrows:hb * vrows + LANE, :] = v.T.astype(bf16)
        vT_ref[0, 0, hb * vrows + LANE:(hb + 1) * vrows, :] = jnp.ones((ONES_ROWS, tm), bf16)
    cx = proj[:, 3 * aw:3 * aw + cw]
    cb = proj[:, 3 * aw + cw:3 * aw + 2 * cw]
    cc = proj[:, 3 * aw + 2 * cw:]
    u_ref[0] = (cc * cx).astype(bf16)
    cb_ref[0] = cb.astype(bf16)


def _attn_kernel(qT_ref, k_ref, vT_ref, lq1_ref, lk1_ref, lq2_ref, lk2_ref, sg_ref, wg_ref, wu_ref, wd_ref,
                 o_ref, wg16_ref, wu16_ref, wd16_ref,
                 qq_ref, pa_ref, pb_ref, acc_ref, m_ref, cm_ref, alpha_ref, jump_ref, *, seq, lpp, hd):
    wg16_ref[...] = wg_ref[...].astype(bf16)
    wu16_ref[...] = wu_ref[...].astype(bf16)
    wd16_ref[...] = wd_ref[...].astype(bf16)
    tq = qT_ref.shape[3]
    n_chunks, tk = vT_ref.shape[1], vT_ref.shape[3]
    col_tiles = [slice(j, j + ATTN_COL_TILE) for j in range(0, 2 * tq, ATTN_COL_TILE)]
    qt = qT_ref[0, 0]
    row = lax.broadcasted_iota(i32, qt.shape, 0)
    zero = jnp.zeros_like(qt)
    qq_ref[:, :tq] = jnp.where(row < hd, qt, zero)
    qq_ref[:, tq:] = jnp.where(row >= hd, qt, zero)

    def key_start(c):
        return c * tk if isinstance(c, int) else pl.multiple_of(c * tk, tk)

    def masked_scores(c, cs, masked):
        s = jnp.dot(k_ref[0, pl.ds(key_start(c), tk), :], qq_ref[:, cs], preferred_element_type=f32)
        if masked:
            krow = key_start(c) + lax.broadcasted_iota(i32, (tk, 1), 0)
            s = jnp.where((krow < seq) | (krow >= lpp - N_META), s, MASK_VALUE)
        return s

    bufs = (pa_ref, pb_ref)

    def probs_tile(c, parity, cs, masked):
        s = masked_scores(c, cs, masked)
        bufs[parity][:, cs] = jnp.exp2(s - m_ref[:, cs]).astype(bf16)
        cm_ref[:, cs] = jnp.max(s, axis=0, keepdims=True)

    def advance(parity):
        m_old = m_ref[...]
        m_new = jnp.maximum(m_old, cm_ref[...])
        jump_ref[...] = jnp.maximum(jump_ref[...], cm_ref[...] - m_old)
        alpha_ref[1 - parity] = jnp.exp2(m_old - m_new)
        m_ref[...] = m_new

    def chunk_step(c, parity, next_masked):
        vt = vT_ref[0, c]
        for cs in col_tiles:
            acc_ref[:, cs] = (alpha_ref[parity, :, cs] * acc_ref[:, cs]
                              + jnp.dot(vt, bufs[parity][:, cs], preferred_element_type=f32))
            if next_masked is not None:
                probs_tile(c + 1, 1 - parity, cs, next_masked)
        if next_masked is not None:
            advance(1 - parity)

    first_masked = seq // tk
    n_pairs = (first_masked - 1) // 2
    acc_ref[...] = jnp.zeros(acc_ref.shape, f32)
    alpha_ref[...] = jnp.ones(alpha_ref.shape, f32)
    jump_ref[...] = jnp.zeros(jump_ref.shape, f32)
    m_ref[...] = jnp.max(jnp.dot(k_ref[0, 0:BF16_ROWS, :], qq_ref[...], preferred_element_type=f32),
                         axis=0, keepdims=True)
    for cs in col_tiles:
        probs_tile(0, 0, cs, first_masked == 0)
    advance(0)

    def pair(i, carry):
        for half in range(2):
            chunk_step(2 * i + half, half, False)
        return carry

    lax.fori_loop(0, n_pairs, pair, 0)
    for c in range(2 * n_pairs, n_chunks):
        chunk_step(c, c % 2, c + 1 >= first_masked if c + 1 < n_chunks else None)

    @pl.when(jnp.max(jump_ref[...]) > EXP2_SAFE_JUMP)
    def _():
        acc_ref[...] = jnp.zeros(acc_ref.shape, f32)
        m_ref[...] = jnp.full(m_ref.shape, MASK_VALUE, f32)

        def exact_chunk(c, carry):
            krow = key_start(c) + lax.broadcasted_iota(i32, (tk, 1), 0)
            valid = (krow < seq) | (krow >= lpp - N_META)
            for cs in col_tiles:
                s = jnp.where(valid, masked_scores(c, cs, False), MASK_VALUE)
                m_old = m_ref[:, cs]
                m_new = jnp.maximum(m_old, jnp.max(s, axis=0, keepdims=True))
                p = jnp.exp2(s - m_new).astype(bf16)
                acc_ref[:, cs] = (jnp.exp2(m_old - m_new) * acc_ref[:, cs]
                                  + jnp.dot(vT_ref[0, c], p, preferred_element_type=f32))
                m_ref[:, cs] = m_new
            return carry

        lax.fori_loop(0, n_chunks, exact_chunk, 0)

    lam = (jnp.exp(jnp.sum(lq1_ref[...] * lk1_ref[...], axis=-1, keepdims=True))
           - jnp.exp(jnp.sum(lq2_ref[...] * lk2_ref[...], axis=-1, keepdims=True)) + LAM_INIT)
    acc = acc_ref[...]
    o = acc[:2 * hd, :] / acc[2 * hd:2 * hd + 1, :]
    d = (o[:, :tq] - lam * o[:, tq:]).T
    o_ref[0] = (_rms(d, sg_ref[...]) * (1.0 - LAM_INIT)).astype(bf16)


def _outproj_kernel(attn_ref, u_ref, up_ref, un_ref, cb_ref, cw_ref, x_ref, tail_ref, wo_ref, g2_ref, wrT_ref,
                    h1_ref, hn2t_ref, affT_ref, *, n_full):
    tm = u_ref.shape[1]
    d_model = x_ref.shape[2]
    u = u_ref[0].astype(f32)
    prev_row = up_ref[0][BF16_ROWS - 1:BF16_ROWS, :].astype(f32)
    next_row = un_ref[0][0:1, :].astype(f32)
    rid = lax.broadcasted_iota(i32, (tm, 1), 0)
    um1 = jnp.where(rid == 0, prev_row, pltpu.roll(u, 1, 0))
    up1 = jnp.where(rid == tm - 1, next_row, pltpu.roll(u, tm - 1, 0))
    w = cw_ref[...]
    conv = cb_ref[0].astype(f32) * (w[0:1] * um1 + w[1:2] * u + w[2:3] * up1)
    mixed = jnp.concatenate([attn_ref[0], conv.astype(bf16)], axis=1)
    h1 = _residual_tile(x_ref, tail_ref, n_full) + jnp.dot(mixed, wo_ref[...], preferred_element_type=f32)
    h1_ref[0] = h1
    hn2 = _rms(h1, g2_ref[...])
    for c in range(d_model // LANE):
        hn2t_ref[0, pl.ds(c, tm, stride=SUBLANE), :] = hn2[:, c * LANE:(c + 1) * LANE]
    logits = lax.dot_general(wrT_ref[...], hn2.astype(bf16), (((1,), (1,)), ((), ())),
                             preferred_element_type=f32)
    p = jnp.exp(logits - jnp.max(logits, axis=0, keepdims=True))
    affT_ref[0] = p / jnp.sum(p, axis=0, keepdims=True)


def _route_kernel(aff_ref, idxT_ref, *, seq, lpp, cap):
    n_exp, ntl = aff_ref.shape[1], aff_ref.shape[2]
    cpad = idxT_ref.shape[1]
    tok = (lax.broadcasted_iota(i32, (ntl, LANE), 0) * LANE + lax.broadcasted_iota(i32, (ntl, LANE), 1))
    valid = (tok < seq) | ((tok >= lpp - N_META) & (tok < lpp))
    a = jnp.where(valid[None], aff_ref[0], -1.0)

    def count(mask):
        return jnp.sum(jnp.sum(mask.astype(f32), axis=1, keepdims=True), axis=2, keepdims=True)

    def bisect(i, ans):
        cand = ans | jnp.left_shift(jnp.int32(1), 30 - i)
        keep = count(a >= lax.bitcast_convert_type(cand, f32)) >= cap
        return jnp.where(keep, cand, ans)

    tau = lax.bitcast_convert_type(lax.fori_loop(0, 31, bisect, jnp.zeros((n_exp, 1, 1), i32)), f32)
    gt = (a > tau).astype(f32)
    eq = (a == tau).astype(f32)
    need = cap - count(a > tau)

    sq0 = lax.broadcasted_iota(i32, (LANE, LANE), 0)
    sq1 = lax.broadcasted_iota(i32, (LANE, LANE), 1)
    tri = (sq0 <= sq1).astype(bf16)
    pick_last = (sq0 == LANE - 1).astype(bf16)
    before = (lax.broadcasted_iota(i32, (ntl, ntl), 1) < lax.broadcasted_iota(i32, (ntl, ntl), 0)).astype(bf16)

    def prefix_count(m):
        local = jnp.dot(m.astype(bf16), tri, preferred_element_type=f32)
        tot = jnp.dot(local.astype(bf16), pick_last, preferred_element_type=f32)
        base = jnp.dot(before, tot.astype(bf16), preferred_element_type=f32)
        return local + base, base, tot

    rr = lax.broadcasted_iota(i32, (cpad, LANE), 0).astype(f32)
    tile_tok0 = (lax.broadcasted_iota(i32, (cpad, LANE), 1) * LANE).astype(f32)
    diag = lax.broadcasted_iota(i32, (ntl, LANE), 0) == lax.broadcasted_iota(i32, (ntl, LANE), 1)
    lane_e = lax.broadcasted_iota(i32, (cpad, n_exp), 1)
    zpad = jnp.zeros((LANE - ntl, LANE), f32)
    out = jnp.zeros((cpad, n_exp), f32)
    is_meta = (tok >= lpp - N_META) & (tok < lpp)
    for e in range(n_exp):
        eq_cum, _, _ = prefix_count(eq[e])
        eq_meta = jnp.sum(jnp.sum(jnp.where(is_meta, eq[e], 0.0), axis=1, keepdims=True), axis=0, keepdims=True)
        eq_all = jnp.sum(jnp.sum(eq[e], axis=1, keepdims=True), axis=0, keepdims=True)
        tie_rank = eq_cum + jnp.where(is_meta, eq_meta - eq_all, eq_meta)
        sel = jnp.maximum(gt[e], eq[e] * (tie_rank <= need[e]).astype(f32))
        cum, base, tot = prefix_count(sel)
        t_lo = jnp.sum(jnp.where(diag, base, 0.0), axis=0, keepdims=True)
        t_hi = jnp.sum(jnp.where(diag, base + tot, 0.0), axis=0, keepdims=True)
        in_tile = (t_lo <= rr) & (t_hi > rr)
        hi = jnp.floor(cum * (1.0 / 32.0))
        lo = cum - 32.0 * hi
        rhs = jnp.concatenate([jnp.concatenate([hi, zpad], axis=0), jnp.concatenate([lo, zpad], axis=0)], axis=1)
        g = jnp.dot(in_tile.astype(f32).astype(bf16), rhs.astype(bf16), preferred_element_type=f32)
        tile_cum = 32.0 * g[:, :LANE] + g[:, LANE:]
        pos = jnp.sum((tile_cum <= rr).astype(f32) + jnp.where(in_tile, tile_tok0, 0.0), axis=1, keepdims=True)
        out = jnp.where(lane_e == e, pos, out)
    rank = lax.broadcasted_iota(i32, (cpad, n_exp), 0)
    idxT_ref[0] = (jnp.where(rank < cap, out, 0.0) * SUBLANE).astype(i32)


def _gather_kernel(idx_ref, src_ref, out_ref):
    cpad = idx_ref.shape[2]

    def body(j, carry):
        for i in range(GATHER_UNROLL):
            r = j * GATHER_UNROLL + i
            out_ref[0, 0, pl.ds(pl.multiple_of(r * SUBLANE, SUBLANE), SUBLANE), :] = (
                src_ref[0, pl.ds(pl.multiple_of(idx_ref[0, 0, r], SUBLANE), SUBLANE), :])
        return carry

    lax.fori_loop(0, cpad // GATHER_UNROLL, body, 0)


def _ffn_kernel(xg_ref, wg_ref, wu_ref, wd_ref, wr_ref, y_ref):
    e = pl.program_id(0)
    row_tile = xg_ref.shape[2] // (SUBLANE * FFN_ROW_TILES)
    d_model = wg_ref.shape[1]
    n_exp = wr_ref.shape[1]
    nc = d_model // LANE
    for rt in range(FFN_ROW_TILES):
        base = rt * row_tile * SUBLANE
        x = jnp.concatenate([xg_ref[0, 0, pl.ds(base + c, row_tile, stride=SUBLANE), :] for c in range(nc)],
                            axis=1).astype(bf16)
        a = jnp.dot(x, wg_ref[0], preferred_element_type=f32)
        u = jnp.dot(x, wu_ref[0], preferred_element_type=f32)
        act = (a * jax.nn.sigmoid(a) * u).astype(bf16)
        y = jnp.dot(act, wd_ref[0], preferred_element_type=f32)
        lg = jnp.dot(x, wr_ref[...], preferred_element_type=f32)
        p = jnp.exp(lg - jnp.max(lg, axis=-1, keepdims=True))
        lane = lax.broadcasted_iota(i32, (row_tile, n_exp), 1)
        gate = jnp.sum(jnp.where(lane == e, p, 0.0), axis=-1, keepdims=True) / jnp.sum(p, axis=-1, keepdims=True)
        y = y * gate
        for c in range(nc):
            y_ref[0, 0, pl.ds(base + c, row_tile, stride=SUBLANE), :] = y[:, c * LANE:(c + 1) * LANE]


def _combine_kernel(idx_ref, y_ref, h1_ref, gf_ref, out_ref, acc_ref, *, cap, n_exp):
    s = pl.program_id(1)
    tt = out_ref.shape[1]
    d_model = out_ref.shape[2]
    zrows = 64 * SUBLANE

    @pl.when(s == 0)
    def _():
        def zero(i, carry):
            acc_ref[pl.ds(pl.multiple_of(i * zrows, zrows), zrows), :] = jnp.zeros((zrows, LANE), f32)
            return carry
        lax.fori_loop(0, acc_ref.shape[0] // zrows, zero, 0)

    def add_rows(ranks):
        offs = [pl.multiple_of(idx_ref[0, 0, r], SUBLANE) for r in ranks]
        vals = [acc_ref[pl.ds(o, SUBLANE), :]
                + y_ref[0, 0, pl.ds(r * SUBLANE if isinstance(r, int) else pl.multiple_of(r * SUBLANE, SUBLANE),
                                    SUBLANE), :]
                for o, r in zip(offs, ranks)]
        for o, v in zip(offs, vals):
            acc_ref[pl.ds(o, SUBLANE), :] = v

    @pl.when(s < n_exp)
    def _():
        def body(j, carry):
            add_rows([j * SCATTER_UNROLL + i for i in range(SCATTER_UNROLL)])
            return carry
        full = cap // SCATTER_UNROLL
        lax.fori_loop(0, full, body, 0)
        if cap % SCATTER_UNROLL:
            add_rows(list(range(full * SCATTER_UNROLL, cap)))

    @pl.when(s >= n_exp)
    def _():
        base = pl.multiple_of((s - n_exp) * (tt * SUBLANE), tt * SUBLANE)
        moe = jnp.concatenate([acc_ref[pl.ds(base + c, tt, stride=SUBLANE), :] for c in range(d_model // LANE)],
                              axis=1)
        out_ref[0] = _rms(h1_ref[0] + moe, gf_ref[...])


def kernel(x, meta_tokens, mix_norm_g, w_in, conv_w, lambda_q1, lambda_k1, lambda_q2, lambda_k2, attn_subln_g,
           w_out, ffn_norm_g, w_router, w_gate, w_up, w_down, final_norm_g):
    batch, seq, d_model = x.shape
    assert w_in.shape[0] == 1, "single-layer block"
    hd = lambda_q1.shape[-1]
    cw = conv_w.shape[-1]
    aw = d_model - cw
    n_heads = aw // (2 * hd)
    n_exp = w_router.shape[-1]
    d_ff = w_gate.shape[-1]
    assert 2 * hd == LANE and aw % LANE == 0 and cw % LANE == 0 and d_model == SUBLANE * LANE
    assert conv_w.shape[1] == 3
    length = N_META + seq
    cap = EC_CAPACITY_FACTOR * length // n_exp
    cpad = _round_up(cap, FFN_ROW_TILES * SUBLANE)
    tm = ROW_TILE
    lpp = _round_up(length + 1, tm)
    nt = lpp // tm
    tt = next(t for t in (512, 256, 128) if seq % t == 0)
    vrows = LANE + ONES_ROWS

    n_full = seq // tm
    assert n_full >= 1 and seq >= BF16_ROWS
    tail = jnp.concatenate([x[:, n_full * tm:], jnp.zeros((batch, lpp - length, d_model), x.dtype),
                            jnp.broadcast_to(meta_tokens.astype(x.dtype)[None], (batch, N_META, d_model))], axis=1)
    x_spec = pl.BlockSpec((1, tm, d_model), lambda b, i: (b, jnp.minimum(i, n_full - 1), 0))
    tail_spec = pl.BlockSpec((1, tm, d_model), lambda b, i: (b, jnp.maximum(i - n_full, 0), 0))
    pos =((jnp.arange(lpp) + N_META) % lpp).astype(f32)
    inv_freq = ROPE_THETA ** (-jnp.arange(0, hd, 2, dtype=f32) / hd)
    ang = pos[:, None] * inv_freq[None, :]
    cos_t = jnp.tile(jnp.cos(ang), (1, LANE // (hd // 2)))
    sin_t = jnp.tile(jnp.concatenate([-jnp.sin(ang), jnp.sin(ang)], axis=-1), (1, LANE // hd))

    row2 = lambda v: v.reshape(1, -1).astype(f32)

    qT, k, vT, u, cb = pl.pallas_call(
        functools.partial(_inproj_kernel, aw=aw, cw=cw, hd=hd, n_full=n_full),
        grid=(batch, nt),
        in_specs=[x_spec, tail_spec,
                  pl.BlockSpec((1, d_model), lambda b, i: (0, 0)),
                  pl.BlockSpec((d_model, 3 * aw + 3 * cw), lambda b, i: (0, 0)),
                  pl.BlockSpec((tm, LANE), lambda b, i: (i, 0)),
                  pl.BlockSpec((tm, LANE), lambda b, i: (i, 0))],
        out_specs=[pl.BlockSpec((1, 1, aw, tm), lambda b, i: (b, i, 0, 0)),
                   pl.BlockSpec((1, tm, aw), lambda b, i: (b, i, 0)),
                   pl.BlockSpec((1, 1, n_heads * vrows, tm), lambda b, i: (b, i, 0, 0)),
                   pl.BlockSpec((1, tm, cw), lambda b, i: (b, i, 0)),
                   pl.BlockSpec((1, tm, cw), lambda b, i: (b, i, 0))],
        out_shape=[jax.ShapeDtypeStruct((batch, nt, aw, tm), bf16),
                   jax.ShapeDtypeStruct((batch, lpp, aw), bf16),
                   jax.ShapeDtypeStruct((batch, nt, n_heads * vrows, tm), bf16),
                   jax.ShapeDtypeStruct((batch, lpp, cw), bf16),
                   jax.ShapeDtypeStruct((batch, lpp, cw), bf16)],
        compiler_params=_params("parallel", "parallel"),
        name="inproj",
    )(x, tail, row2(mix_norm_g[0]), w_in[0].astype(bf16), cos_t, sin_t)

    attn_steps = batch * n_heads * nt
    n_slabs = 1 << (attn_steps.bit_length() - 1)
    assert (n_exp * d_model) % (n_slabs * BF16_ROWS) == 0 and (n_exp * d_ff) % (n_slabs * BF16_ROWS) == 0
    slab_map = lambda b, hh, i: (jnp.minimum((b * n_heads + hh) * nt + i, n_slabs - 1), 0)
    up_slab = pl.BlockSpec((n_exp * d_model // n_slabs, d_ff), slab_map)
    down_slab = pl.BlockSpec((n_exp * d_ff // n_slabs, d_model), slab_map)

    lam_spec = pl.BlockSpec((1, hd), lambda b, hh, i: (0, 0))
    attn, wg16, wu16, wd16 = pl.pallas_call(
        functools.partial(_attn_kernel, seq=seq, lpp=lpp, hd=hd),
        grid=(batch, n_heads, nt),
        in_specs=[pl.BlockSpec((1, 1, LANE, tm), lambda b, hh, i: (b, i, hh, 0)),
                  pl.BlockSpec((1, lpp, LANE), lambda b, hh, i: (b, 0, hh)),
                  pl.BlockSpec((1, nt, vrows, tm), lambda b, hh, i: (b, 0, hh, 0)),
                  lam_spec, lam_spec, lam_spec, lam_spec,
                  pl.BlockSpec((1, LANE), lambda b, hh, i: (0, 0)),
                  up_slab, up_slab, down_slab],
        out_specs=[pl.BlockSpec((1, tm, LANE), lambda b, hh, i: (b, i, hh)), up_slab, up_slab, down_slab],
        out_shape=[jax.ShapeDtypeStruct((batch, lpp, aw), bf16),
                   jax.ShapeDtypeStruct((n_exp * d_model, d_ff), bf16),
                   jax.ShapeDtypeStruct((n_exp * d_model, d_ff), bf16),
                   jax.ShapeDtypeStruct((n_exp * d_ff, d_model), bf16)],
        scratch_shapes=[pltpu.VMEM((LANE, 2 * tm), bf16), pltpu.VMEM((tm, 2 * tm), bf16),
                        pltpu.VMEM((tm, 2 * tm), bf16), pltpu.VMEM((vrows, 2 * tm), f32),
                        pltpu.VMEM((1, 2 * tm), f32), pltpu.VMEM((1, 2 * tm), f32),
                        pltpu.VMEM((2, 1, 2 * tm), f32), pltpu.VMEM((1, 2 * tm), f32)],
        compiler_params=_params("arbitrary", "arbitrary", "arbitrary"),
        name="diff_attn",
    )(qT, k, vT, row2(lambda_q1[0]), row2(lambda_k1[0]), row2(lambda_q2[0]), row2(lambda_k2[0]),
      row2(attn_subln_g[0]), w_gate[0].reshape(n_exp * d_model, d_ff), w_up[0].reshape(n_exp * d_model, d_ff),
      w_down[0].reshape(n_exp * d_ff, d_model))

    halo = tm // BF16_ROWS
    n_halo = lpp // BF16_ROWS
    h1, hn2t, affT = pl.pallas_call(
        functools.partial(_outproj_kernel, n_full=n_full),
        grid=(batch, nt),
        in_specs=[pl.BlockSpec((1, tm, aw), lambda b, i: (b, i, 0)),
                  pl.BlockSpec((1, tm, cw), lambda b, i: (b, i, 0)),
                  pl.BlockSpec((1, BF16_ROWS, cw), lambda b, i: (b, (i * halo + n_halo - 1) % n_halo, 0)),
                  pl.BlockSpec((1, BF16_ROWS, cw), lambda b, i: (b, ((i + 1) * halo) % n_halo, 0)),
                  pl.BlockSpec((1, tm, cw), lambda b, i: (b, i, 0)),
                  pl.BlockSpec((3, cw), lambda b, i: (0, 0)),
                  x_spec, tail_spec,
                  pl.BlockSpec((aw + cw, d_model), lambda b, i: (0, 0)),
                  pl.BlockSpec((1, d_model), lambda b, i: (0, 0)),
                  pl.BlockSpec((n_exp, d_model), lambda b, i: (0, 0))],
        out_specs=[pl.BlockSpec((1, tm, d_model), lambda b, i: (b, i, 0)),
                   pl.BlockSpec((1, tm * SUBLANE, LANE), lambda b, i: (b, i, 0)),
                   pl.BlockSpec((1, n_exp, tm), lambda b, i: (b, 0, i))],
        out_shape=[jax.ShapeDtypeStruct((batch, lpp, d_model), f32),
                   jax.ShapeDtypeStruct((batch, lpp * SUBLANE, LANE), f32),
                   jax.ShapeDtypeStruct((batch, n_exp, lpp), f32)],
        compiler_params=_params("parallel", "parallel"),
        name="outproj_router",
    )(attn, u, u, u, cb, conv_w[0].astype(f32), x, tail, w_out[0].astype(bf16), row2(ffn_norm_g[0]),
      w_router[0].T.astype(bf16))

    ntl = _round_up(lpp // LANE, SUBLANE)
    assert ntl <= LANE
    aff_tiles = jnp.pad(affT, ((0, 0), (0, 0), (0, ntl * LANE - lpp))).reshape(batch, n_exp, ntl, LANE)
    idxT = pl.pallas_call(
        functools.partial(_route_kernel, seq=seq, lpp=lpp, cap=cap),
        grid=(batch,),
        in_specs=[pl.BlockSpec((1, n_exp, ntl, LANE), lambda b: (b, 0, 0, 0))],
        out_specs=pl.BlockSpec((1, cpad, n_exp), lambda b: (b, 0, 0)),
        out_shape=jax.ShapeDtypeStruct((batch, cpad, n_exp), i32),
        compiler_params=_params("parallel"),
        name="expert_choice",
    )(aff_tiles)
    idx = jnp.swapaxes(idxT, 1, 2).reshape(batch * n_exp, 1, cpad)

    idx_spec = lambda im: pl.BlockSpec((1, 1, cpad), im, memory_space=pltpu.SMEM)
    xg = pl.pallas_call(
        _gather_kernel,
        grid=(batch, n_exp),
        in_specs=[idx_spec(lambda b, e: (b * n_exp + e, 0, 0)),
                  pl.BlockSpec((1, lpp * SUBLANE, LANE), lambda b, e: (b, 0, 0), pipeline_mode=pl.Buffered(1))],
        out_specs=pl.BlockSpec((1, 1, cpad * SUBLANE, LANE), lambda b, e: (b, e, 0, 0)),
        out_shape=jax.ShapeDtypeStruct((batch, n_exp, cpad * SUBLANE, LANE), f32),
        compiler_params=_params("parallel", "arbitrary"),
        name="token_gather",
    )(idx, hn2t)

    y = pl.pallas_call(
        _ffn_kernel,
        grid=(n_exp, batch),
        in_specs=[pl.BlockSpec((1, 1, cpad * SUBLANE, LANE), lambda e, b: (b, e, 0, 0)),
                  pl.BlockSpec((1, d_model, d_ff), lambda e, b: (e, 0, 0)),
                  pl.BlockSpec((1, d_model, d_ff), lambda e, b: (e, 0, 0)),
                  pl.BlockSpec((1, d_ff, d_model), lambda e, b: (e, 0, 0)),
                  pl.BlockSpec((d_model, n_exp), lambda e, b: (0, 0))],
        out_specs=pl.BlockSpec((1, 1, cpad * SUBLANE, LANE), lambda e, b: (b, e, 0, 0)),
        out_shape=jax.ShapeDtypeStruct((batch, n_exp, cpad * SUBLANE, LANE), f32),
        compiler_params=_params("parallel", "parallel"),
        name="expert_ffn",
    )(xg, wg16.reshape(n_exp, d_model, d_ff), wu16.reshape(n_exp, d_model, d_ff),
      wd16.reshape(n_exp, d_ff, d_model), w_router[0].astype(bf16))

    n_out = seq // tt
    last = n_exp - 1
    out = pl.pallas_call(
        functools.partial(_combine_kernel, cap=cap, n_exp=n_exp),
        grid=(batch, n_exp + n_out),
        in_specs=[idx_spec(lambda b, s: (b * n_exp + jnp.minimum(s, last), 0, 0)),
                  pl.BlockSpec((1, 1, cpad * SUBLANE, LANE), lambda b, s: (b, jnp.minimum(s, last), 0, 0)),
                  pl.BlockSpec((1, tt, d_model), lambda b, s: (b, jnp.maximum(s - n_exp, 0), 0)),
                  pl.BlockSpec((1, d_model), lambda b, s: (0, 0))],
        out_specs=pl.BlockSpec((1, tt, d_model), lambda b, s: (b, jnp.maximum(s - n_exp, 0), 0)),
        out_shape=jax.ShapeDtypeStruct((batch, seq, d_model), x.dtype),
        scratch_shapes=[pltpu.VMEM((lpp * SUBLANE, LANE), f32)],
        compiler_params=_params("parallel", "arbitrary"),
        name="combine_norm",
    )(idx, y, h1, row2(final_norm_g))
    return out
```

```python
import functools

import jax
import jax.numpy as jnp
from jax import lax
from jax.experimental import pallas as pl
from jax.experimental.pallas import tpu as pltpu

N_META = 16
EC_CAPACITY_FACTOR = 2
ROPE_THETA = 10000.0
NORM_EPS = 1e-6
LAM_INIT = 0.8 - 0.6 * 1.0

LANE = 128
SUBLANE = 8
BF16_ROWS = 16
VMEM_LIMIT = 56 * 1024 * 1024

ROW_TILE = 768
MASK_VALUE = -1e30
GATHER_UNROLL = 8
SCATTER_UNROLL = 16
ONES_ROWS = BF16_ROWS
FFN_ROW_TILES = 5
FFN_F_CHUNKS = 2
ATTN_COL_TILE = 256
LOG2_E = 1.4426950408889634
EXP2_SAFE_JUMP = 100.0

f32 = jnp.float32
bf16 = jnp.bfloat16
i32 = jnp.int32


def _round_up(a, m):
    return -(-a // m) * m


def _params(*sem):
    return pltpu.CompilerParams(dimension_semantics=sem, vmem_limit_bytes=VMEM_LIMIT)


def _rms(x, g):
    return x * lax.rsqrt(jnp.mean(x * x, axis=-1, keepdims=True) + NORM_EPS) * g


def _residual_tile(x_ref, tail_ref, n_full):
    return jnp.where(pl.program_id(1) < n_full, x_ref[0], tail_ref[0])


def _inproj_kernel(x_ref, tail_ref, g_ref, w_ref, cos_ref, sin_ref, qT_ref, k_ref, vT_ref, u_ref, cb_ref,
                   *, aw, cw, hd, n_full):
    tm = x_ref.shape[1]
    hn = _rms(_residual_tile(x_ref, tail_ref, n_full), g_ref[...]).astype(bf16)
    proj = jnp.dot(hn, w_ref[...], preferred_element_type=f32)
    cos = cos_ref[...]
    sin = sin_ref[...]
    lane = lax.broadcasted_iota(i32, (tm, LANE), 1)
    first_half = (lane % hd) < (hd // 2)

    def rope(x):
        rot = jnp.where(first_half, pltpu.roll(x, LANE - hd // 2, 1), pltpu.roll(x, hd // 2, 1))
        return x * cos + rot * sin

    scale = hd ** -0.5 * LOG2_E
    vrows = LANE + ONES_ROWS
    for hb in range(aw // LANE):
        sl = slice(hb * LANE, (hb + 1) * LANE)
        q = rope(proj[:, sl]) * scale
        qT_ref[0, 0, sl, :] = q.T.astype(bf16)
        k = rope(proj[:, aw + hb * LANE:aw + (hb + 1) * LANE])
        k_ref[0, :, sl] = k.astype(bf16)
        v = proj[:, 2 * aw + hb * LANE:2 * aw + (hb + 1) * LANE]
        vT_ref[0, 0, hb * vrows:hb * vrows + LANE, :] = v.T.astype(bf16)
        vT_ref[0, 0, hb * vrows + LANE:(hb + 1) * vrows, :] = jnp.ones((ONES_ROWS, tm), bf16)
    cx = proj[:, 3 * aw:3 * aw + cw]
    cb = proj[:, 3 * aw + cw:3 * aw + 2 * cw]
    cc = proj[:, 3 * aw + 2 * cw:]
    u_ref[0] = (cc * cx).astype(bf16)
    cb_ref[0] = cb.astype(bf16)


def _attn_kernel(qT_ref, k_ref, vT_ref, lq1_ref, lk1_ref, lq2_ref, lk2_ref, sg_ref, wg_ref, wu_ref, wd_ref,
                 o_ref, wg16_ref, wu16_ref, wd16_ref,
                 qq_ref, pa_ref, pb_ref, acc_ref, m_ref, cm_ref, alpha_ref, jump_ref, *, seq, lpp, hd):
    wg16_ref[...] = wg_ref[...].astype(bf16)
    wu16_ref[...] = wu_ref[...].astype(bf16)
    wd16_ref[...] = wd_ref[...].astype(bf16)
    tq = qT_ref.shape[3]
    n_chunks, tk = vT_ref.shape[1], vT_ref.shape[3]
    col_tiles = [slice(j, j + ATTN_COL_TILE) for j in range(0, 2 * tq, ATTN_COL_TILE)]
    qt = qT_ref[0, 0]
    row = lax.broadcasted_iota(i32, qt.shape, 0)
    zero = jnp.zeros_like(qt)
    qq_ref[:, :tq] = jnp.where(row < hd, qt, zero)
    qq_ref[:, tq:] = jnp.where(row >= hd, qt, zero)

    def key_start(c):
        return c * tk if isinstance(c, int) else pl.multiple_of(c * tk, tk)

    def masked_scores(c, cs, masked):
        s = jnp.dot(k_ref[0, pl.ds(key_start(c), tk), :], qq_ref[:, cs], preferred_element_type=f32)
        if masked:
            krow = key_start(c) + lax.broadcasted_iota(i32, (tk, 1), 0)
            s = jnp.where((krow < seq) | (krow >= lpp - N_META), s, MASK_VALUE)
        return s

    bufs = (pa_ref, pb_ref)

    def probs_tile(c, parity, cs, masked):
        s = masked_scores(c, cs, masked)
        bufs[parity][:, cs] = jnp.exp2(s - m_ref[:, cs]).astype(bf16)
        cm_ref[:, cs] = jnp.max(s, axis=0, keepdims=True)

    def advance(parity):
        m_old = m_ref[...]
        m_new = jnp.maximum(m_old, cm_ref[...])
        jump_ref[...] = jnp.maximum(jump_ref[...], cm_ref[...] - m_old)
        alpha_ref[1 - parity] = jnp.exp2(m_old - m_new)
        m_ref[...] = m_new

    def chunk_step(c, parity, next_masked):
        vt = vT_ref[0, c]
        for cs in col_tiles:
            acc_ref[:, cs] = (alpha_ref[parity, :, cs] * acc_ref[:, cs]
                              + jnp.dot(vt, bufs[parity][:, cs], preferred_element_type=f32))
            if next_masked is not None:
                probs_tile(c + 1, 1 - parity, cs, next_masked)
        if next_masked is not None:
            advance(1 - parity)

    first_masked = seq // tk
    n_pairs = (first_masked - 1) // 2
    acc_ref[...] = jnp.zeros(acc_ref.shape, f32)
    alpha_ref[...] = jnp.ones(alpha_ref.shape, f32)
    jump_ref[...] = jnp.zeros(jump_ref.shape, f32)
    m_ref[...] = jnp.max(jnp.dot(k_ref[0, 0:BF16_ROWS, :], qq_ref[...], preferred_element_type=f32),
                         axis=0, keepdims=True)
    for cs in col_tiles:
        probs_tile(0, 0, cs, first_masked == 0)
    advance(0)

    def pair(i, carry):
        for half in range(2):
            chunk_step(2 * i + half, half, False)
        return carry

    lax.fori_loop(0, n_pairs, pair, 0)
    for c in range(2 * n_pairs, n_chunks):
        chunk_step(c, c % 2, c + 1 >= first_masked if c + 1 < n_chunks else None)

    @pl.when(jnp.max(jump_ref[...]) > EXP2_SAFE_JUMP)
    def _():
        acc_ref[...] = jnp.zeros(acc_ref.shape, f32)
        m_ref[...] = jnp.full(m_ref.shape, MASK_VALUE, f32)

        def exact_chunk(c, carry):
            krow = key_start(c) + lax.broadcasted_iota(i32, (tk, 1), 0)
            valid = (krow < seq) | (krow >= lpp - N_META)
            for cs in col_tiles:
                s = jnp.where(valid, masked_scores(c, cs, False), MASK_VALUE)
                m_old = m_ref[:, cs]
                m_new = jnp.maximum(m_old, jnp.max(s, axis=0, keepdims=True))
                p = jnp.exp2(s - m_new).astype(bf16)
                acc_ref[:, cs] = (jnp.exp2(m_old - m_new) * acc_ref[:, cs]
                                  + jnp.dot(vT_ref[0, c], p, preferred_element_type=f32))
                m_ref[:, cs] = m_new
            return carry

        lax.fori_loop(0, n_chunks, exact_chunk, 0)

    lam = (jnp.exp(jnp.sum(lq1_ref[...] * lk1_ref[...], axis=-1, keepdims=True))
           - jnp.exp(jnp.sum(lq2_ref[...] * lk2_ref[...], axis=-1, keepdims=True)) + LAM_INIT)
    acc = acc_ref[...]
    o = acc[:2 * hd, :] / acc[2 * hd:2 * hd + 1, :]
    d = (o[:, :tq] - lam * o[:, tq:]).T
    o_ref[0] = (_rms(d, sg_ref[...]) * (1.0 - LAM_INIT)).astype(bf16)


def _outproj_kernel(attn_ref, u_ref, up_ref, un_ref, cb_ref, cw_ref, x_ref, tail_ref, wo_ref, g2_ref, wrT_ref,
                    h1_ref, hn2t_ref, affT_ref, *, n_full):
    tm = u_ref.shape[1]
    d_model = x_ref.shape[2]
    u = u_ref[0].astype(f32)
    prev_row = up_ref[0][BF16_ROWS - 1:BF16_ROWS, :].astype(f32)
    next_row = un_ref[0][0:1, :].astype(f32)
    rid = lax.broadcasted_iota(i32, (tm, 1), 0)
    um1 = jnp.where(rid == 0, prev_row, pltpu.roll(u, 1, 0))
    up1 = jnp.where(rid == tm - 1, next_row, pltpu.roll(u, tm - 1, 0))
    w = cw_ref[...]
    conv = cb_ref[0].astype(f32) * (w[0:1] * um1 + w[1:2] * u + w[2:3] * up1)
    mixed = jnp.concatenate([attn_ref[0], conv.astype(bf16)], axis=1)
    h1 = _residual_tile(x_ref, tail_ref, n_full) + jnp.dot(mixed, wo_ref[...], preferred_element_type=f32)
    h1_ref[0] = h1
    hn2 = _rms(h1, g2_ref[...])
    for c in range(d_model // LANE):
        hn2t_ref[0, pl.ds(c, tm, stride=SUBLANE), :] = hn2[:, c * LANE:(c + 1) * LANE]
    logits = lax.dot_general(wrT_ref[...], hn2.astype(bf16), (((1,), (1,)), ((), ())),
                             preferred_element_type=f32)
    p = jnp.exp(logits - jnp.max(logits, axis=0, keepdims=True))
    affT_ref[0] = p / jnp.sum(p, axis=0, keepdims=True)


def _route_kernel(aff_ref, idxT_ref, *, seq, lpp, cap):
    n_exp, ntl = aff_ref.shape[1], aff_ref.shape[2]
    cpad = idxT_ref.shape[1]
    tok = (lax.broadcasted_iota(i32, (ntl, LANE), 0) * LANE + lax.broadcasted_iota(i32, (ntl, LANE), 1))
    valid = (tok < seq) | ((tok >= lpp - N_META) & (tok < lpp))
    a = jnp.where(valid[None], aff_ref[0], -1.0)

    def count(mask):
        return jnp.sum(jnp.sum(mask.astype(f32), axis=1, keepdims=True), axis=2, keepdims=True)

    def bisect(i, ans):
        cand = ans | jnp.left_shift(jnp.int32(1), 30 - i)
        keep = count(a >= lax.bitcast_convert_type(cand, f32)) >= cap
        return jnp.where(keep, cand, ans)

    tau = lax.bitcast_convert_type(lax.fori_loop(0, 31, bisect, jnp.zeros((n_exp, 1, 1), i32)), f32)
    gt = (a > tau).astype(f32)
    eq = (a == tau).astype(f32)
    need = cap - count(a > tau)

    sq0 = lax.broadcasted_iota(i32, (LANE, LANE), 0)
    sq1 = lax.broadcasted_iota(i32, (LANE, LANE), 1)
    tri = (sq0 <= sq1).astype(bf16)
    pick_last = (sq0 == LANE - 1).astype(bf16)
    before = (lax.broadcasted_iota(i32, (ntl, ntl), 1) < lax.broadcasted_iota(i32, (ntl, ntl), 0)).astype(bf16)

    def prefix_count(m):
        local = jnp.dot(m.astype(bf16), tri, preferred_element_type=f32)
        tot = jnp.dot(local.astype(bf16), pick_last, preferred_element_type=f32)
        base = jnp.dot(before, tot.astype(bf16), preferred_element_type=f32)
        return local + base, base, tot

    rr = lax.broadcasted_iota(i32, (cpad, LANE), 0).astype(f32)
    tile_tok0 = (lax.broadcasted_iota(i32, (cpad, LANE), 1) * LANE).astype(f32)
    diag = lax.broadcasted_iota(i32, (ntl, LANE), 0) == lax.broadcasted_iota(i32, (ntl, LANE), 1)
    lane_e = lax.broadcasted_iota(i32, (cpad, n_exp), 1)
    zpad = jnp.zeros((LANE - ntl, LANE), f32)
    out = jnp.zeros((cpad, n_exp), f32)
    is_meta = (tok >= lpp - N_META) & (tok < lpp)
    for e in range(n_exp):
        eq_cum, _, _ = prefix_count(eq[e])
        eq_meta = jnp.sum(jnp.sum(jnp.where(is_meta, eq[e], 0.0), axis=1, keepdims=True), axis=0, keepdims=True)
        eq_all = jnp.sum(jnp.sum(eq[e], axis=1, keepdims=True), axis=0, keepdims=True)
        tie_rank = eq_cum + jnp.where(is_meta, eq_meta - eq_all, eq_meta)
        sel = jnp.maximum(gt[e], eq[e] * (tie_rank <= need[e]).astype(f32))
        cum, base, tot = prefix_count(sel)
        t_lo = jnp.sum(jnp.where(diag, base, 0.0), axis=0, keepdims=True)
        t_hi = jnp.sum(jnp.where(diag, base + tot, 0.0), axis=0, keepdims=True)
        in_tile = (t_lo <= rr) & (t_hi > rr)
        hi = jnp.floor(cum * (1.0 / 32.0))
        lo = cum - 32.0 * hi
        rhs = jnp.concatenate([jnp.concatenate([hi, zpad], axis=0), jnp.concatenate([lo, zpad], axis=0)], axis=1)
        g = jnp.dot(in_tile.astype(f32).astype(bf16), rhs.astype(bf16), preferred_element_type=f32)
        tile_cum = 32.0 * g[:, :LANE] + g[:, LANE:]
        pos = jnp.sum((tile_cum <= rr).astype(f32) + jnp.where(in_tile, tile_tok0, 0.0), axis=1, keepdims=True)
        out = jnp.where(lane_e == e, pos, out)
    rank = lax.broadcasted_iota(i32, (cpad, n_exp), 0)
    idxT_ref[0] = (jnp.where(rank < cap, out, 0.0) * SUBLANE).astype(i32)


def _gather_kernel(idx_ref, src_ref, out_ref):
    cpad = idx_ref.shape[2]

    def body(j, carry):
        for i in range(GATHER_UNROLL):
            r = j * GATHER_UNROLL + i
            out_ref[0, 0, pl.ds(pl.multiple_of(r * SUBLANE, SUBLANE), SUBLANE), :] = (
                src_ref[0, pl.ds(pl.multiple_of(idx_ref[0, 0, r], SUBLANE), SUBLANE), :])
        return carry

    lax.fori_loop(0, cpad // GATHER_UNROLL, body, 0)


def _ffn_kernel(xg_ref, wg_ref, wu_ref, wd_ref, wr_ref, y_ref):
    e = pl.program_id(0)
    row_tile = xg_ref.shape[2] // (SUBLANE * FFN_ROW_TILES)
    d_model = wg_ref.shape[1]
    n_exp = wr_ref.shape[1]
    nc = d_model // LANE
    d_ff = wg_ref.shape[2]
    f_chunks = [slice(f, f + d_ff // FFN_F_CHUNKS) for f in range(0, d_ff, d_ff // FFN_F_CHUNKS)]
    for rt in range(FFN_ROW_TILES):
        base = rt * row_tile * SUBLANE
        x = jnp.concatenate([xg_ref[0, 0, pl.ds(base + c, row_tile, stride=SUBLANE), :] for c in range(nc)],
                            axis=1).astype(bf16)
        y = None
        for fs in f_chunks:
            a = jnp.dot(x, wg_ref[0, :, fs], preferred_element_type=f32)
            u = jnp.dot(x, wu_ref[0, :, fs], preferred_element_type=f32)
            act = (a * jax.nn.sigmoid(a) * u).astype(bf16)
            part = jnp.dot(act, wd_ref[0, fs, :], preferred_element_type=f32)
            y = part if y is None else y + part
        lg = jnp.dot(x, wr_ref[...], preferred_element_type=f32)
        p = jnp.exp(lg - jnp.max(lg, axis=-1, keepdims=True))
        lane = lax.broadcasted_iota(i32, (row_tile, n_exp), 1)
        gate = jnp.sum(jnp.where(lane == e, p, 0.0), axis=-1, keepdims=True) / jnp.sum(p, axis=-1, keepdims=True)
        y = y * gate
        for c in range(nc):
            y_ref[0, 0, pl.ds(base + c, row_tile, stride=SUBLANE), :] = y[:, c * LANE:(c + 1) * LANE]


def _combine_kernel(idx_ref, y_ref, h1_ref, gf_ref, out_ref, acc_ref, *, cap, n_exp):
    s = pl.program_id(1)
    tt = out_ref.shape[1]
    d_model = out_ref.shape[2]
    zrows = 64 * SUBLANE

    @pl.when(s == 0)
    def _():
        def zero(i, carry):
            acc_ref[pl.ds(pl.multiple_of(i * zrows, zrows), zrows), :] = jnp.zeros((zrows, LANE), f32)
            return carry
        lax.fori_loop(0, acc_ref.shape[0] // zrows, zero, 0)

    def add_rows(ranks):
        offs = [pl.multiple_of(idx_ref[0, 0, r], SUBLANE) for r in ranks]
        vals = [acc_ref[pl.ds(o, SUBLANE), :]
                + y_ref[0, 0, pl.ds(r * SUBLANE if isinstance(r, int) else pl.multiple_of(r * SUBLANE, SUBLANE),
                                    SUBLANE), :]
                for o, r in zip(offs, ranks)]
        for o, v in zip(offs, vals):
            acc_ref[pl.ds(o, SUBLANE), :] = v

    @pl.when(s < n_exp)
    def _():
        def body(j, carry):
            add_rows([j * SCATTER_UNROLL + i for i in range(SCATTER_UNROLL)])
            return carry
        full = cap // SCATTER_UNROLL
        lax.fori_loop(0, full, body, 0)
        if cap % SCATTER_UNROLL:
            add_rows(list(range(full * SCATTER_UNROLL, cap)))

    @pl.when(s >= n_exp)
    def _():
        base = pl.multiple_of((s - n_exp) * (tt * SUBLANE), tt * SUBLANE)
        moe = jnp.concatenate([acc_ref[pl.ds(base + c, tt, stride=SUBLANE), :] for c in range(d_model // LANE)],
                              axis=1)
        out_ref[0] = _rms(h1_ref[0] + moe, gf_ref[...])


def kernel(x, meta_tokens, mix_norm_g, w_in, conv_w, lambda_q1, lambda_k1, lambda_q2, lambda_k2, attn_subln_g,
           w_out, ffn_norm_g, w_router, w_gate, w_up, w_down, final_norm_g):
    batch, seq, d_model = x.shape
    assert w_in.shape[0] == 1, "single-layer block"
    hd = lambda_q1.shape[-1]
    cw = conv_w.shape[-1]
    aw = d_model - cw
    n_heads = aw // (2 * hd)
    n_exp = w_router.shape[-1]
    d_ff = w_gate.shape[-1]
    assert 2 * hd == LANE and aw % LANE == 0 and cw % LANE == 0 and d_model == SUBLANE * LANE
    assert conv_w.shape[1] == 3
    length = N_META + seq
    cap = EC_CAPACITY_FACTOR * length // n_exp
    cpad = _round_up(cap, FFN_ROW_TILES * SUBLANE)
    tm = ROW_TILE
    lpp = _round_up(length + 1, tm)
    nt = lpp // tm
    tt = next(t for t in (512, 256, 128) if seq % t == 0)
    vrows = LANE + ONES_ROWS

    n_full = seq // tm
    assert n_full >= 1 and seq >= BF16_ROWS
    tail = jnp.concatenate([x[:, n_full * tm:], jnp.zeros((batch, lpp - length, d_model), x.dtype),
                            jnp.broadcast_to(meta_tokens.astype(x.dtype)[None], (batch, N_META, d_model))], axis=1)
    x_spec = pl.BlockSpec((1, tm, d_model), lambda b, i: (b, jnp.minimum(i, n_full - 1), 0))
    tail_spec = pl.BlockSpec((1, tm, d_model), lambda b, i: (b, jnp.maximum(i - n_full, 0), 0))
    pos =((jnp.arange(lpp) + N_META) % lpp).astype(f32)
    inv_freq = ROPE_THETA ** (-jnp.arange(0, hd, 2, dtype=f32) / hd)
    ang = pos[:, None] * inv_freq[None, :]
    cos_t = jnp.tile(jnp.cos(ang), (1, LANE // (hd // 2)))
    sin_t = jnp.tile(jnp.concatenate([-jnp.sin(ang), jnp.sin(ang)], axis=-1), (1, LANE // hd))

    row2 = lambda v: v.reshape(1, -1).astype(f32)

    qT, k, vT, u, cb = pl.pallas_call(
        functools.partial(_inproj_kernel, aw=aw, cw=cw, hd=hd, n_full=n_full),
        grid=(batch, nt),
        in_specs=[x_spec, tail_spec,
                  pl.BlockSpec((1, d_model), lambda b, i: (0, 0)),
                  pl.BlockSpec((d_model, 3 * aw + 3 * cw), lambda b, i: (0, 0)),
                  pl.BlockSpec((tm, LANE), lambda b, i: (i, 0)),
                  pl.BlockSpec((tm, LANE), lambda b, i: (i, 0))],
        out_specs=[pl.BlockSpec((1, 1, aw, tm), lambda b, i: (b, i, 0, 0)),
                   pl.BlockSpec((1, tm, aw), lambda b, i: (b, i, 0)),
                   pl.BlockSpec((1, 1, n_heads * vrows, tm), lambda b, i: (b, i, 0, 0)),
                   pl.BlockSpec((1, tm, cw), lambda b, i: (b, i, 0)),
                   pl.BlockSpec((1, tm, cw), lambda b, i: (b, i, 0))],
        out_shape=[jax.ShapeDtypeStruct((batch, nt, aw, tm), bf16),
                   jax.ShapeDtypeStruct((batch, lpp, aw), bf16),
                   jax.ShapeDtypeStruct((batch, nt, n_heads * vrows, tm), bf16),
                   jax.ShapeDtypeStruct((batch, lpp, cw), bf16),
                   jax.ShapeDtypeStruct((batch, lpp, cw), bf16)],
        compiler_params=_params("parallel", "parallel"),
        name="inproj",
    )(x, tail, row2(mix_norm_g[0]), w_in[0].astype(bf16), cos_t, sin_t)

    attn_steps = batch * n_heads * nt
    n_slabs = 1 << (attn_steps.bit_length() - 1)
    assert (n_exp * d_model) % (n_slabs * BF16_ROWS) == 0 and (n_exp * d_ff) % (n_slabs * BF16_ROWS) == 0
    slab_map = lambda b, hh, i: (jnp.minimum((b * n_heads + hh) * nt + i, n_slabs - 1), 0)
    up_slab = pl.BlockSpec((n_exp * d_model // n_slabs, d_ff), slab_map)
    down_slab = pl.BlockSpec((n_exp * d_ff // n_slabs, d_model), slab_map)

    lam_spec = pl.BlockSpec((1, hd), lambda b, hh, i: (0, 0))
    attn, wg16, wu16, wd16 = pl.pallas_call(
        functools.partial(_attn_kernel, seq=seq, lpp=lpp, hd=hd),
        grid=(batch, n_heads, nt),
        in_specs=[pl.BlockSpec((1, 1, LANE, tm), lambda b, hh, i: (b, i, hh, 0)),
                  pl.BlockSpec((1, lpp, LANE), lambda b, hh, i: (b, 0, hh)),
                  pl.BlockSpec((1, nt, vrows, tm), lambda b, hh, i: (b, 0, hh, 0)),
                  lam_spec, lam_spec, lam_spec, lam_spec,
                  pl.BlockSpec((1, LANE), lambda b, hh, i: (0, 0)),
                  up_slab, up_slab, down_slab],
        out_specs=[pl.BlockSpec((1, tm, LANE), lambda b, hh, i: (b, i, hh)), up_slab, up_slab, down_slab],
        out_shape=[jax.ShapeDtypeStruct((batch, lpp, aw), bf16),
                   jax.ShapeDtypeStruct((n_exp * d_model, d_ff), bf16),
                   jax.ShapeDtypeStruct((n_exp * d_model, d_ff), bf16),
                   jax.ShapeDtypeStruct((n_exp * d_ff, d_model), bf16)],
        scratch_shapes=[pltpu.VMEM((LANE, 2 * tm), bf16), pltpu.VMEM((tm, 2 * tm), bf16),
                        pltpu.VMEM((tm, 2 * tm), bf16), pltpu.VMEM((vrows, 2 * tm), f32),
                        pltpu.VMEM((1, 2 * tm), f32), pltpu.VMEM((1, 2 * tm), f32),
                        pltpu.VMEM((2, 1, 2 * tm), f32), pltpu.VMEM((1, 2 * tm), f32)],
        compiler_params=_params("arbitrary", "arbitrary", "arbitrary"),
        name="diff_attn",
    )(qT, k, vT, row2(lambda_q1[0]), row2(lambda_k1[0]), row2(lambda_q2[0]), row2(lambda_k2[0]),
      row2(attn_subln_g[0]), w_gate[0].reshape(n_exp * d_model, d_ff), w_up[0].reshape(n_exp * d_model, d_ff),
      w_down[0].reshape(n_exp * d_ff, d_model))

    halo = tm // BF16_ROWS
    n_halo = lpp // BF16_ROWS
    h1, hn2t, affT = pl.pallas_call(
        functools.partial(_outproj_kernel, n_full=n_full),
        grid=(batch, nt),
        in_specs=[pl.BlockSpec((1, tm, aw), lambda b, i: (b, i, 0)),
                  pl.BlockSpec((1, tm, cw), lambda b, i: (b, i, 0)),
                  pl.BlockSpec((1, BF16_ROWS, cw), lambda b, i: (b, (i * halo + n_halo - 1) % n_halo, 0)),
                  pl.BlockSpec((1, BF16_ROWS, cw), lambda b, i: (b, ((i + 1) * halo) % n_halo, 0)),
                  pl.BlockSpec((1, tm, cw), lambda b, i: (b, i, 0)),
                  pl.BlockSpec((3, cw), lambda b, i: (0, 0)),
                  x_spec, tail_spec,
                  pl.BlockSpec((aw + cw, d_model), lambda b, i: (0, 0)),
                  pl.BlockSpec((1, d_model), lambda b, i: (0, 0)),
                  pl.BlockSpec((n_exp, d_model), lambda b, i: (0, 0))],
        out_specs=[pl.BlockSpec((1, tm, d_model), lambda b, i: (b, i, 0)),
                   pl.BlockSpec((1, tm * SUBLANE, LANE), lambda b, i: (b, i, 0)),
                   pl.BlockSpec((1, n_exp, tm), lambda b, i: (b, 0, i))],
        out_shape=[jax.ShapeDtypeStruct((batch, lpp, d_model), f32),
                   jax.ShapeDtypeStruct((batch, lpp * SUBLANE, LANE), f32),
                   jax.ShapeDtypeStruct((batch, n_exp, lpp), f32)],
        compiler_params=_params("parallel", "parallel"),
        name="outproj_router",
    )(attn, u, u, u, cb, conv_w[0].astype(f32), x, tail, w_out[0].astype(bf16), row2(ffn_norm_g[0]),
      w_router[0].T.astype(bf16))

    ntl = _round_up(lpp // LANE, SUBLANE)
    assert ntl <= LANE
    aff_tiles = jnp.pad(affT, ((0, 0), (0, 0), (0, ntl * LANE - lpp))).reshape(batch, n_exp, ntl, LANE)
    idxT = pl.pallas_call(
        functools.partial(_route_kernel, seq=seq, lpp=lpp, cap=cap),
        grid=(batch,),
        in_specs=[pl.BlockSpec((1, n_exp, ntl, LANE), lambda b: (b, 0, 0, 0))],
        out_specs=pl.BlockSpec((1, cpad, n_exp), lambda b: (b, 0, 0)),
        out_shape=jax.ShapeDtypeStruct((batch, cpad, n_exp), i32),
        compiler_params=_params("parallel"),
        name="expert_choice",
    )(aff_tiles)
    idx = jnp.swapaxes(idxT, 1, 2).reshape(batch * n_exp, 1, cpad)

    idx_spec = lambda im: pl.BlockSpec((1, 1, cpad), im, memory_space=pltpu.SMEM)
    xg = pl.pallas_call(
        _gather_kernel,
        grid=(batch, n_exp),
        in_specs=[idx_spec(lambda b, e: (b * n_exp + e, 0, 0)),
                  pl.BlockSpec((1, lpp * SUBLANE, LANE), lambda b, e: (b, 0, 0), pipeline_mode=pl.Buffered(1))],
        out_specs=pl.BlockSpec((1, 1, cpad * SUBLANE, LANE), lambda b, e: (b, e, 0, 0)),
        out_shape=jax.ShapeDtypeStruct((batch, n_exp, cpad * SUBLANE, LANE), f32),
        compiler_params=_params("parallel", "arbitrary"),
        name="token_gather",
    )(idx, hn2t)

    y = pl.pallas_call(
        _ffn_kernel,
        grid=(n_exp, batch),
        in_specs=[pl.BlockSpec((1, 1, cpad * SUBLANE, LANE), lambda e, b: (b, e, 0, 0)),
                  pl.BlockSpec((1, d_model, d_ff), lambda e, b: (e, 0, 0)),
                  pl.BlockSpec((1, d_model, d_ff), lambda e, b: (e, 0, 0)),
                  pl.BlockSpec((1, d_ff, d_model), lambda e, b: (e, 0, 0)),
                  pl.BlockSpec((d_model, n_exp), lambda e, b: (0, 0))],
        out_specs=pl.BlockSpec((1, 1, cpad * SUBLANE, LANE), lambda e, b: (b, e, 0, 0)),
        out_shape=jax.ShapeDtypeStruct((batch, n_exp, cpad * SUBLANE, LANE), f32),
        compiler_params=_params("parallel", "parallel"),
        name="expert_ffn",
    )(xg, wg16.reshape(n_exp, d_model, d_ff), wu16.reshape(n_exp, d_model, d_ff),
      wd16.reshape(n_exp, d_ff, d_model), w_router[0].astype(bf16))

    n_out = seq // tt
    last = n_exp - 1
    out = pl.pallas_call(
        functools.partial(_combine_kernel, cap=cap, n_exp=n_exp),
        grid=(batch, n_exp + n_out),
        in_specs=[idx_spec(lambda b, s: (b * n_exp + jnp.minimum(s, last), 0, 0)),
                  pl.BlockSpec((1, 1, cpad * SUBLANE, LANE), lambda b, s: (b, jnp.minimum(s, last), 0, 0)),
                  pl.BlockSpec((1, tt, d_model), lambda b, s: (b, jnp.maximum(s - n_exp, 0), 0)),
                  pl.BlockSpec((1, d_model), lambda b, s: (0, 0))],
        out_specs=pl.BlockSpec((1, tt, d_model), lambda b, s: (b, jnp.maximum(s - n_exp, 0), 0)),
        out_shape=jax.ShapeDtypeStruct((batch, seq, d_model), x.dtype),
        scratch_shapes=[pltpu.VMEM((lpp * SUBLANE, LANE), f32)],
        compiler_params=_params("parallel", "arbitrary"),
        name="combine_norm",
    )(idx, y, h1, row2(final_norm_g))
    return out
```
